```python
import jax, jax.numpy as jnp
from jax import lax
import numpy as np

D_MODEL = 1024
BATCH = 16
SEQ = 2048
DEPTH = 2
DEC_BATCH = 8
DEC_SEQ = 32
PAST_LEN = 2048

CHUNK = 64
HEAD_DIM = 64
H_A = 8
H_B = 8
LEFT_CHUNKS_A = 8
REL_CLIP_A = 128
N_REL_A = 2 * REL_CLIP_A + 1
SB_BLOCK = 128
W_A = H_A * HEAD_DIM
W_B = H_B * HEAD_DIM
W_AB = W_A + W_B
IN_AB = 4 * W_A + 4 * W_B
AB_SPLITS = (W_A, 2 * W_A, 3 * W_A, 4 * W_A, 4 * W_A + W_B, 4 * W_A + 2 * W_B, 4 * W_A + 3 * W_B)
H_C = 16
KV_C = 4
G_C = H_C // KV_C
WINDOW_C = 128
LEFT_CHUNKS_C = WINDOW_C // CHUNK
W_C = H_C * HEAD_DIM
KVW_C = KV_C * HEAD_DIM
IN_C = 2 * W_C + 2 * KVW_C
C_SPLITS = (W_C, W_C + KVW_C, W_C + 2 * KVW_C)
ROPE_THETA = 10000.0
RMS_EPS = 1e-6
NEG_INF = -1e30
N_AB = (DEPTH + 1) // 2
N_C = DEPTH // 2

kernel_name = 'hybrid_chunk_streaming_encoder_step'


def _rmsnorm(x, g):
    xf = x.astype(jnp.float32)
    r = lax.rsqrt(jnp.mean(xf * xf, axis=-1, keepdims=True) + RMS_EPS)
    return (xf * r * g.astype(jnp.float32)).astype(x.dtype)


def _rope(x, pos):
    half = HEAD_DIM // 2
    inv = ROPE_THETA ** (-jnp.arange(half, dtype=jnp.float32) * (2.0 / HEAD_DIM))
    ang = pos.astype(jnp.float32)[:, None] * inv[None, :]
    cos, sin = jnp.cos(ang)[:, None, :], jnp.sin(ang)[:, None, :]
    xf = x.astype(jnp.float32)
    x1, x2 = xf[..., :half], xf[..., half:]
    return jnp.concatenate([x1 * cos - x2 * sin, x2 * cos + x1 * sin], axis=-1).astype(x.dtype)


def _chunk_band(x, n_left):
    B, S = x.shape[:2]
    nC = S // CHUNK
    xp = jnp.pad(x, ((0, 0), (n_left * CHUNK, 0), (0, 0), (0, 0)))
    xp = xp.reshape(B, nC + n_left, CHUNK, *x.shape[2:])
    return jnp.concatenate([xp[:, i:i + nC] for i in range(n_left + 1)], axis=2)


def _chunk_band_mask(q_pos, k_pos, n_left):
    qc = q_pos[:, :, None] // CHUNK
    kc = k_pos[:, None, :] // CHUNK
    return (k_pos[:, None, :] >= 0) & (kc <= qc) & (kc >= qc - n_left)


def _band_attention(q, k, v, mask, bias=None, sinks=None):
    s = jnp.einsum('bcqhgd,bckhd->bchgqk', q, k).astype(jnp.float32) * (HEAD_DIM ** -0.5)
    if bias is not None:
        s = s + bias
    s = jnp.where(mask[None, :, None, None], s, NEG_INF)
    m = jnp.max(s, axis=-1, keepdims=True)
    if sinks is not None:
        sk = sinks.astype(jnp.float32)[:, :, None, None]
        m = jnp.maximum(m, sk)
        p = jnp.exp(s - m)
        denom = jnp.sum(p, axis=-1, keepdims=True) + jnp.exp(sk - m)
    else:
        p = jnp.exp(s - m)
        denom = jnp.sum(p, axis=-1, keepdims=True)
    return jnp.einsum('bchgqk,bckhd->bcqhgd', (p / denom).astype(v.dtype), v)


def _band_prompt(q, k, v, pos, n_left, groups):
    B, S, Hq, d = q.shape
    nC = S // CHUNK
    qc = q.reshape(B, nC, CHUNK, Hq // groups, groups, d)
    q_pos = pos.reshape(nC, CHUNK)
    k_pos = (jnp.arange(nC) * CHUNK)[:, None] + jnp.arange(-n_left * CHUNK, CHUNK)[None, :]
    return qc, _chunk_band(k, n_left), _chunk_band(v, n_left), q_pos, k_pos


def _band_sample(q, k, v, ck, cv, past, groups):
    B, T, Hq, d = q.shape
    L = ck.shape[1]
    qc = q.reshape(B, 1, T, Hq // groups, groups, d)
    k_all = jnp.concatenate([ck, k], axis=1)[:, None]
    v_all = jnp.concatenate([cv, v], axis=1)[:, None]
    q_pos = (past + jnp.arange(T))[None]
    k_pos = (past - L + jnp.arange(L + T))[None]
    return qc, k_all, v_all, q_pos, k_pos


def _mixer_a(q, k, v, q_pos, k_pos, table):
    mask = _chunk_band_mask(q_pos, k_pos, LEFT_CHUNKS_A)
    rel = jnp.clip(q_pos[:, :, None] - k_pos[:, None, :], -REL_CLIP_A, REL_CLIP_A) + REL_CLIP_A
    bias = jnp.moveaxis(jnp.take(table, rel, axis=1), 0, 1)[:, :, None].astype(jnp.float32)
    return _band_attention(q, k, v, mask, bias=bias)


def _mixer_c(q, k, v, q_pos, k_pos, sinks):
    mask = _chunk_band_mask(q_pos, k_pos, LEFT_CHUNKS_C)
    return _band_attention(q, k, v, mask, sinks=sinks.reshape(KV_C, G_C))


def _sb_block(q, k, v, q_pos, k_pos):
    z = jnp.einsum('bqhd,bkhd->bhqk', q, k).astype(jnp.float32) * (HEAD_DIM ** -0.5)
    causal = k_pos[None, :] < q_pos[:, None]
    log_1m = jnp.where(causal, jax.nn.log_sigmoid(-z), 0.0)
    later = lax.cumsum(log_1m, axis=3, reverse=True) - log_1m
    a = jnp.where(causal, jnp.exp(jax.nn.log_sigmoid(z) + later), 0.0)
    return jnp.einsum('bhqk,bkhd->bqhd', a.astype(v.dtype), v)


def _stick_breaking_prompt(q, k, v):
    B, S, H, d = q.shape
    nb = S // SB_BLOCK
    pos = jnp.arange(S)
    qb = q.reshape(B, nb, SB_BLOCK, H, d).transpose(1, 0, 2, 3, 4)
    pb = pos.reshape(nb, SB_BLOCK)
    out = lax.map(lambda a: _sb_block(a[0], k, v, a[1], pos), (qb, pb))
    return out.transpose(1, 0, 2, 3, 4).reshape(B, S, H, d)


def _merge(x, parts, w_out, g_post):
    y = jnp.concatenate(parts, axis=-1) @ w_out
    return x + _rmsnorm(y, g_post)


def _ab_project(x, g_pre, w_in):
    B, S, _ = x.shape
    u = _rmsnorm(x, g_pre) @ w_in
    qa, ka, va, ga, qb, kb, vb, gb = jnp.split(u, AB_SPLITS, axis=-1)
    hd = lambda t, h: t.reshape(B, S, h, HEAD_DIM)
    return hd(qa, H_A), hd(ka, H_A), hd(va, H_A), ga, hd(qb, H_B), hd(kb, H_B), hd(vb, H_B), gb


def _c_project(x, g_pre, w_in):
    B, S, _ = x.shape
    u = _rmsnorm(x, g_pre) @ w_in
    q, k, v, g = jnp.split(u, C_SPLITS, axis=-1)
    return (q.reshape(B, S, H_C, HEAD_DIM), k.reshape(B, S, KV_C, HEAD_DIM),
            v.reshape(B, S, KV_C, HEAD_DIM), g)


def _ab_layer_prompt(x, g_pre, w_in, w_out, g_post, rel_table):
    B, S, _ = x.shape
    qa, ka, va, ga, qb, kb, vb, gb = _ab_project(x, g_pre, w_in)
    pos = jnp.arange(S)
    oa = _mixer_a(*_band_prompt(qa, ka, va, pos, LEFT_CHUNKS_A, 1), rel_table).reshape(B, S, W_A)
    ob = _stick_breaking_prompt(qb, kb, vb).reshape(B, S, W_B)
    y = _merge(x, [oa * jax.nn.silu(ga), ob * jax.nn.silu(gb)], w_out, g_post)
    la = min(LEFT_CHUNKS_A * CHUNK, S)
    return y, (ka[:, S - la:], va[:, S - la:], kb, vb)


def _ab_layer_sample(x, ca_k, ca_v, cb_k, cb_v, past, g_pre, w_in, w_out, g_post, rel_table):
    B, T, _ = x.shape
    qa, ka, va, ga, qb, kb, vb, gb = _ab_project(x, g_pre, w_in)
    oa = _mixer_a(*_band_sample(qa, ka, va, ca_k, ca_v, past, 1), rel_table).reshape(B, T, W_A)
    kb_all = jnp.concatenate([cb_k, kb], axis=1)
    vb_all = jnp.concatenate([cb_v, vb], axis=1)
    ob = _sb_block(qb, kb_all, vb_all, past + jnp.arange(T), jnp.arange(past + T)).reshape(B, T, W_B)
    y = _merge(x, [oa * jax.nn.silu(ga), ob * jax.nn.silu(gb)], w_out, g_post)
    return y, (ka, va, kb, vb)


def _c_layer_prompt(x, g_pre, w_in, sinks, w_out, g_post):
    B, S, _ = x.shape
    q, k, v, g = _c_project(x, g_pre, w_in)
    pos = jnp.arange(S)
    q, k = _rope(q, pos), _rope(k, pos)
    o = _mixer_c(*_band_prompt(q, k, v, pos, LEFT_CHUNKS_C, G_C), sinks).reshape(B, S, W_C)
    y = _merge(x, [o * jax.nn.silu(g)], w_out, g_post)
    lc = min(WINDOW_C, S)
    return y, (k[:, S - lc:], v[:, S - lc:])


def _c_layer_sample(x, cc_k, cc_v, past, g_pre, w_in, sinks, w_out, g_post):
    B, T, _ = x.shape
    q, k, v, g = _c_project(x, g_pre, w_in)
    pos = past + jnp.arange(T)
    q, k = _rope(q, pos), _rope(k, pos)
    o = _mixer_c(*_band_sample(q, k, v, cc_k, cc_v, past, G_C), sinks).reshape(B, T, W_C)
    y = _merge(x, [o * jax.nn.silu(g)], w_out, g_post)
    return y, (k, v)


def setup_inputs(seed: int = 0) -> dict:
    key = jax.random.key(seed)
    ks = jax.random.split(key, 20)
    f32 = jnp.float32
    nrm = lambda k, shape, scale=1.0: scale * jax.random.normal(k, shape, f32)
    la = min(LEFT_CHUNKS_A * CHUNK, PAST_LEN)
    lc = min(WINDOW_C, PAST_LEN)
    return {
        'x_prompt': nrm(ks[0], (BATCH, SEQ, D_MODEL)),
        'x_sample': nrm(ks[1], (DEC_BATCH, DEC_SEQ, D_MODEL)),
        'cache_a_k': nrm(ks[2], (N_AB, DEC_BATCH, la, H_A, HEAD_DIM)),
        'cache_a_v': nrm(ks[3], (N_AB, DEC_BATCH, la, H_A, HEAD_DIM)),
        'cache_b_k': nrm(ks[4], (N_AB, DEC_BATCH, PAST_LEN, H_B, HEAD_DIM)),
        'cache_b_v': nrm(ks[5], (N_AB, DEC_BATCH, PAST_LEN, H_B, HEAD_DIM)),
        'cache_c_k': nrm(ks[6], (N_C, DEC_BATCH, lc, KV_C, HEAD_DIM)),
        'cache_c_v': nrm(ks[7], (N_C, DEC_BATCH, lc, KV_C, HEAD_DIM)),
        'ab_norm_pre': 1.0 + nrm(ks[8], (N_AB, D_MODEL), 0.1),
        'ab_w_in': nrm(ks[9], (N_AB, D_MODEL, IN_AB), D_MODEL ** -0.5),
        'ab_w_out': nrm(ks[10], (N_AB, W_AB, D_MODEL), W_AB ** -0.5),
        'ab_norm_post': 1.0 + nrm(ks[11], (N_AB, D_MODEL), 0.1),
        'a_rel_bias': nrm(ks[12], (N_AB, H_A, N_REL_A), 0.5),
        'c_norm_pre': 1.0 + nrm(ks[13], (N_C, D_MODEL), 0.1),
        'c_w_in': nrm(ks[14], (N_C, D_MODEL, IN_C), D_MODEL ** -0.5),
        'c_sinks': nrm(ks[15], (N_C, H_C)),
        'c_w_out': nrm(ks[16], (N_C, W_C, D_MODEL), W_C ** -0.5),
        'c_norm_post': 1.0 + nrm(ks[17], (N_C, D_MODEL), 0.1),
    }


def reference(x_prompt, x_sample, cache_a_k, cache_a_v, cache_b_k, cache_b_v, cache_c_k, cache_c_v,
              ab_norm_pre, ab_w_in, ab_w_out, ab_norm_post, a_rel_bias,
              c_norm_pre, c_w_in, c_sinks, c_w_out, c_norm_post):
    past = cache_b_k.shape[2]
    yp, ys = x_prompt, x_sample
    ab_p, ab_s, c_p, c_s = [], [], [], []
    for layer in range(DEPTH):
        i = layer // 2
        if layer % 2 == 0:
            yp, rows = _ab_layer_prompt(yp, ab_norm_pre[i], ab_w_in[i], ab_w_out[i], ab_norm_post[i], a_rel_bias[i])
            ab_p.append(rows)
            ys, rows = _ab_layer_sample(ys, cache_a_k[i], cache_a_v[i], cache_b_k[i], cache_b_v[i], past,
                                        ab_norm_pre[i], ab_w_in[i], ab_w_out[i], ab_norm_post[i], a_rel_bias[i])
            ab_s.append(rows)
        else:
            yp, rows = _c_layer_prompt(yp, c_norm_pre[i], c_w_in[i], c_sinks[i], c_w_out[i], c_norm_post[i])
            c_p.append(rows)
            ys, rows = _c_layer_sample(ys, cache_c_k[i], cache_c_v[i], past,
                                       c_norm_pre[i], c_w_in[i], c_sinks[i], c_w_out[i], c_norm_post[i])
            c_s.append(rows)
    st = lambda rows, j: jnp.stack([r[j] for r in rows])
    return (yp, ys,
            st(ab_p, 0), st(ab_p, 1), st(ab_p, 2), st(ab_p, 3), st(c_p, 0), st(c_p, 1),
            st(ab_s, 0), st(ab_s, 1), st(ab_s, 2), st(ab_s, 3), st(c_s, 0), st(c_s, 1))
```

```python
import functools

import jax
import jax.numpy as jnp
from jax import lax
from jax.experimental import pallas as pl
from jax.experimental.pallas import tpu as pltpu

D_MODEL = 1024
CHUNK = 64
HEAD_DIM = 64
H_A = 8
H_B = 8
LEFT_CHUNKS_A = 8
REL_CLIP_A = 128
N_REL_A = 2 * REL_CLIP_A + 1
W_A = H_A * HEAD_DIM
W_B = H_B * HEAD_DIM
H_C = 16
KV_C = 4
G_C = H_C // KV_C
WINDOW_C = 128
W_C = H_C * HEAD_DIM
KVW_C = KV_C * HEAD_DIM
ROPE_THETA = 10000.0
RMS_EPS = 1e-6
NEG_INF = -1e30
SCALE = HEAD_DIM ** -0.5

LANES = 128
QROWS = 128
A_WIN = (LEFT_CHUNKS_A + 2) * CHUNK
A_TAB = 384
A_ROLLW = A_WIN + QROWS
C_WIN = WINDOW_C + QROWS
SB_BLK = 256
VMEM_LIMIT = 48 * 1024 * 1024

F32 = jnp.float32
BF16 = jnp.bfloat16


def _params(n_axes):
    return pltpu.CompilerParams(dimension_semantics=("parallel",) * n_axes,
                                vmem_limit_bytes=VMEM_LIMIT)


def _rms_scale(x):
    return lax.rsqrt(jnp.mean(x * x, axis=-1, keepdims=True) + RMS_EPS)


def _silu(u):
    return u / (1.0 + jnp.exp(-u))


def _low_half(shape):
    return lax.broadcasted_iota(jnp.int32, shape, 1) < HEAD_DIM


def _head_of_pair(x_bf16, low, h):
    xf = x_bf16.astype(F32)
    return jnp.where(low if h == 0 else jnp.logical_not(low), xf, 0.0).astype(BF16)


def _dot_nt(a, b):
    return lax.dot_general(a, b, (((1,), (1,)), ((), ())), preferred_element_type=F32)


def _proj_ab_kernel(x_ref, g_ref, w_ref, qa, ka, va, sga, qb, kb, vb, sgb, kaf, vaf, kbf, vbf):
    x = x_ref[...]
    xn = (x * _rms_scale(x) * g_ref[...]).astype(BF16)

    def col(j):
        return jnp.dot(xn, w_ref[:, j * W_A:(j + 1) * W_A], preferred_element_type=F32)

    qa[...] = (col(0) * SCALE).astype(BF16)
    u = col(1); kaf[...] = u; ka[...] = u.astype(BF16)
    u = col(2); vaf[...] = u; va[...] = u.astype(BF16)
    sga[...] = _silu(col(3)).astype(BF16)
    qb[...] = (col(4) * SCALE).astype(BF16)
    u = col(5); kbf[...] = u; kb[...] = u.astype(BF16)
    u = col(6); vbf[...] = u; vb[...] = u.astype(BF16)
    sgb[...] = _silu(col(7)).astype(BF16)


def _proj_ab(x2d, g_pre, w_bf16, tm):
    T = x2d.shape[0]
    row = lambda w: pl.BlockSpec((tm, w), lambda i: (i, 0))
    full = lambda a: pl.BlockSpec(a.shape, lambda i: (0, 0))
    bshape = jax.ShapeDtypeStruct((T, W_A), BF16)
    fshape = jax.ShapeDtypeStruct((T, W_A), F32)
    return pl.pallas_call(
        _proj_ab_kernel,
        grid=(T // tm,),
        in_specs=[row(D_MODEL), full(g_pre), full(w_bf16)],
        out_specs=[row(W_A)] * 12,
        out_shape=[bshape] * 8 + [fshape] * 4,
        compiler_params=_params(1),
        name="proj_ab",
    )(x2d, g_pre, w_bf16)


def _merge_kernel(*refs, n_parts):
    h_refs = refs[:n_parts]
    w_ref, g_ref, x_ref, y_ref = refs[n_parts:]
    acc = None
    off = 0
    for h in h_refs:
        k = h.shape[1]
        d = jnp.dot(h[...], w_ref[off:off + k, :], preferred_element_type=F32)
        acc = d if acc is None else acc + d
        off += k
    y_ref[...] = x_ref[...] + acc * _rms_scale(acc) * g_ref[...]


def _merge(parts, w_bf16, g_post, x2d, tm):
    T = x2d.shape[0]
    row = lambda w: pl.BlockSpec((tm, w), lambda i: (i, 0))
    full = lambda a: pl.BlockSpec(a.shape, lambda i: (0, 0))
    return pl.pallas_call(
        functools.partial(_merge_kernel, n_parts=len(parts)),
        grid=(T // tm,),
        in_specs=[row(p.shape[1]) for p in parts] + [full(w_bf16), full(g_post), row(D_MODEL)],
        out_specs=row(D_MODEL),
        out_shape=jax.ShapeDtypeStruct((T, D_MODEL), F32),
        compiler_params=_params(1),
        name="merge",
    )(*parts, w_bf16, g_post, x2d)


def _bias_kernel(tab_ref, out_ref, *, n_valid):
    m = lax.broadcasted_iota(jnp.int32, (A_TAB, A_ROLLW), 1)
    t = lax.broadcasted_iota(jnp.int32, (A_TAB, A_ROLLW), 0)
    idx = jnp.where(m <= A_WIN, jnp.clip(A_WIN - QROWS - m, -REL_CLIP_A, REL_CLIP_A) + REL_CLIP_A,
                    2 * REL_CLIP_A)
    onehot = jnp.where(t == idx, 1.0, 0.0).astype(BF16)
    tab = tab_ref[...]
    hi = tab.astype(BF16)
    r1 = tab - hi.astype(F32)
    mid = r1.astype(BF16)
    lo = (r1 - mid.astype(F32)).astype(BF16)
    src = (jnp.dot(hi, onehot, preferred_element_type=F32)
           + jnp.dot(mid, onehot, preferred_element_type=F32)
           + jnp.dot(lo, onehot, preferred_element_type=F32))
    i = lax.broadcasted_iota(jnp.int32, (QROWS, A_WIN), 0)
    j = lax.broadcasted_iota(jnp.int32, (QROWS, A_WIN), 1)
    qc = LEFT_CHUNKS_A + i // CHUNK
    kc = j // CHUNK
    valid = (kc <= qc) & (kc >= qc - LEFT_CHUNKS_A) & (j < n_valid)
    for h in range(H_A):
        row = jnp.broadcast_to(src[h:h + 1, :], (QROWS, A_ROLLW))
        toep = pltpu.roll(row, 0, 1, stride=1, stride_axis=0)
        out_ref[h] = jnp.where(valid, toep[:, :A_WIN], NEG_INF)


def _bias_table(table, n_valid):
    tab = jnp.pad(table, ((0, 0), (0, A_TAB - N_REL_A)))
    return pl.pallas_call(
        functools.partial(_bias_kernel, n_valid=n_valid),
        out_shape=jax.ShapeDtypeStruct((H_A, QROWS, A_WIN), F32),
        name="a_bias",
    )(tab)


def _attn_a_step(q_ref, k_ref, v_ref, sg_ref, bias_ref, o_ref, q0, k0, width, bcol0):
    qm = q_ref[pl.ds(q0, QROWS), :]
    kw = k_ref[pl.ds(k0, width), :]
    vw = v_ref[pl.ds(k0, width), :]
    low = _low_half((QROWS, LANES))
    outs = []
    for h in range(2):
        s = _dot_nt(_head_of_pair(qm, low, h), kw) + bias_ref[h, :, bcol0:bcol0 + width]
        p = jnp.exp(s - jnp.max(s, axis=-1, keepdims=True))
        den = jnp.sum(p, axis=-1, keepdims=True)
        outs.append(jnp.dot(p.astype(BF16), vw, preferred_element_type=F32) / den)
    o = jnp.where(low, outs[0], outs[1])
    o_ref[pl.ds(q0, QROWS), :] = (o * sg_ref[pl.ds(q0, QROWS), :].astype(F32)).astype(BF16)


def _attn_a_kernel(q_ref, k_ref, v_ref, sg_ref, bias_ref, o_ref, *, n_steps):
    lead = A_WIN // QROWS - 1
    refs = (q_ref, k_ref, v_ref, sg_ref, bias_ref, o_ref)
    for m in range(min(lead, n_steps)):
        _attn_a_step(*refs, m * QROWS, 0, (m + 1) * QROWS, (lead - m) * QROWS)

    def body(m, c):
        _attn_a_step(*refs, pl.multiple_of(m * QROWS, QROWS),
                     pl.multiple_of((m - lead) * QROWS, QROWS), A_WIN, 0)
        return c

    if n_steps > lead:
        lax.fori_loop(lead, n_steps, body, 0)


def _attn_a_window_kernel(q_ref, k_ref, v_ref, sg_ref, bias_ref, o_ref):
    _attn_a_step(q_ref, k_ref, v_ref, sg_ref, bias_ref, o_ref, 0, 0, A_WIN, 0)


def _attn_a(q, k, v, sg, bias, window):
    B, Sq, _ = q.shape
    Sk = k.shape[1]
    qspec = pl.BlockSpec((None, Sq, LANES), lambda p, b: (b, 0, p))
    kspec = pl.BlockSpec((None, Sk, LANES), lambda p, b: (b, 0, p))
    bspec = pl.BlockSpec((2, QROWS, A_WIN), lambda p, b: (p, 0, 0))
    kern = _attn_a_window_kernel if window else functools.partial(_attn_a_kernel, n_steps=Sq // QROWS)
    return pl.pallas_call(
        kern,
        grid=(W_A // LANES, B),
        in_specs=[qspec, kspec, kspec, qspec, bspec],
        out_specs=qspec,
        out_shape=jax.ShapeDtypeStruct(q.shape, BF16),
        compiler_params=_params(2),
        name="attn_a_window" if window else "attn_a",
    )(q, k, v, sg, bias)


def _sb_block(qh, kj, vj, carry, acc, tri, causal):
    z = _dot_nt(qh, kj)
    sp = jnp.maximum(z, 0.0) + jnp.log(1.0 + jnp.exp(-jnp.abs(z)))
    l1m = -sp
    if causal is not None:
        l1m = jnp.where(causal, l1m, 0.0)
    hi = l1m.astype(BF16)
    lo = (l1m - hi.astype(F32)).astype(BF16)
    later = (jnp.dot(hi, tri, preferred_element_type=F32)
             + jnp.dot(lo, tri, preferred_element_type=F32)) + carry
    a = jnp.exp(z - sp + later)
    if causal is not None:
        a = jnp.where(causal, a, 0.0)
    acc = acc + jnp.dot(a.astype(BF16), vj, preferred_element_type=F32)
    carry = carry + jnp.sum(l1m, axis=-1, keepdims=True)
    return carry, acc


def _sb_consts():
    r = lax.broadcasted_iota(jnp.int32, (SB_BLK, SB_BLK), 0)
    c = lax.broadcasted_iota(jnp.int32, (SB_BLK, SB_BLK), 1)
    tri = jnp.where(r > c, 1.0, 0.0).astype(BF16)
    causal = c < r
    return tri, causal


def _sb_query_block(qm, kd, vd, kp_ref, vp_ref, n_past, tri, causal, low):
    outs = []
    for h in range(2):
        qh = _head_of_pair(qm, low, h)
        carry = jnp.zeros((SB_BLK, 1), F32)
        acc = jnp.zeros((SB_BLK, LANES), F32)
        carry, acc = _sb_block(qh, kd, vd, carry, acc, tri, causal)

        def body(t, st, qh=qh):
            j = n_past - 1 - t
            r0 = pl.multiple_of(j * SB_BLK, SB_BLK)
            kj = kp_ref[pl.ds(r0, SB_BLK), :].astype(BF16)
            vj = vp_ref[pl.ds(r0, SB_BLK), :].astype(BF16)
            return _sb_block(qh, kj, vj, st[0], st[1], tri, None)

        carry, acc = lax.fori_loop(0, n_past, body, (carry, acc))
        outs.append(acc)
    return jnp.where(low, outs[0], outs[1])


def _attn_b_kernel(q_ref, k_ref, v_ref, sg_ref, o_ref):
    tri, causal = _sb_consts()
    low = _low_half((SB_BLK, LANES))

    def qblock(i, c):
        r0 = pl.multiple_of(i * SB_BLK, SB_BLK)
        rows = pl.ds(r0, SB_BLK)
        o = _sb_query_block(q_ref[rows, :], k_ref[rows, :], v_ref[rows, :], k_ref, v_ref, i,
                            tri, causal, low)
        o_ref[rows, :] = (o * sg_ref[rows, :].astype(F32)).astype(BF16)
        return c

    lax.fori_loop(0, q_ref.shape[0] // SB_BLK, qblock, 0)


def _attn_b_cached_kernel(q_ref, kd_ref, vd_ref, kp_ref, vp_ref, sg_ref, o_ref):
    tri, causal = _sb_consts()
    low = _low_half((SB_BLK, LANES))
    o = _sb_query_block(q_ref[...], kd_ref[...], vd_ref[...], kp_ref, vp_ref,
                        kp_ref.shape[0] // SB_BLK, tri, causal, low)
    o_ref[...] = (o * sg_ref[...].astype(F32)).astype(BF16)


def _attn_b(q, k, v, sg):
    B, S, _ = q.shape
    spec = pl.BlockSpec((None, S, LANES), lambda p, b: (b, 0, p))
    return pl.pallas_call(
        _attn_b_kernel,
        grid=(W_B // LANES, B),
        in_specs=[spec] * 4,
        out_specs=spec,
        out_shape=jax.ShapeDtypeStruct(q.shape, BF16),
        compiler_params=_params(2),
        name="attn_b",
    )(q, k, v, sg)


def _attn_b_cached(q, kd, vd, kp, vp, sg):
    B = q.shape[0]
    P = kp.shape[1]
    spec = pl.BlockSpec((None, SB_BLK, LANES), lambda p, b: (b, 0, p))
    pspec = pl.BlockSpec((None, P, LANES), lambda p, b: (b, 0, p))
    return pl.pallas_call(
        _attn_b_cached_kernel,
        grid=(W_B // LANES, B),
        in_specs=[spec, spec, spec, pspec, pspec, spec],
        out_specs=spec,
        out_shape=jax.ShapeDtypeStruct(q.shape, BF16),
        compiler_params=_params(2),
        name="attn_b_cached",
    )(q, kd, vd, kp, vp, sg)


def _rope_table_kernel(inv_ref, cos_ref, sin_ref, *, pos0, period):
    shape = cos_ref.shape
    row = lax.broadcasted_iota(jnp.int32, shape, 0)
    lane = lax.broadcasted_iota(jnp.int32, shape, 1)
    ang = (pos0 + row % period).astype(F32) * inv_ref[...]
    cos_ref[...] = jnp.cos(ang)
    s = jnp.sin(ang)
    sin_ref[...] = jnp.where(lane % HEAD_DIM < HEAD_DIM // 2, -s, s)


def _rope_tables(n_rows, pos0, period):
    half = HEAD_DIM // 2
    inv = ROPE_THETA ** (-jnp.arange(half, dtype=F32) * (2.0 / HEAD_DIM))
    inv = jnp.tile(inv, LANES // half)[None, :]
    shape = jax.ShapeDtypeStruct((n_rows, LANES), F32)
    return pl.pallas_call(
        functools.partial(_rope_table_kernel, pos0=pos0, period=period),
        out_shape=[shape, shape],
        name="rope_table",
    )(inv)


def _proj_c_kernel(x_ref, g_ref, w_ref, cos_ref, sin_ref, q_o, k_o, v_o, sg_o, kf_o, vf_o):
    x = x_ref[...]
    xn = (x * _rms_scale(x) * g_ref[...]).astype(BF16)
    cos = cos_ref[...]
    sin = sin_ref[...]
    first = lax.broadcasted_iota(jnp.int32, cos.shape, 1) % HEAD_DIM < HEAD_DIM // 2

    def rope(u):
        partner = jnp.where(first, pltpu.roll(u, LANES - HEAD_DIM // 2, 1), pltpu.roll(u, HEAD_DIM // 2, 1))
        return u * cos + partner * sin

    blk = 4 * LANES
    for c0 in range(0, W_C, blk):
        u = jnp.dot(xn, w_ref[:, c0:c0 + blk], preferred_element_type=F32)
        for j in range(0, blk, LANES):
            q_o[:, c0 + j:c0 + j + LANES] = (rope(u[:, j:j + LANES]) * SCALE).astype(BF16)
    u = jnp.dot(xn, w_ref[:, W_C:W_C + KVW_C], preferred_element_type=F32)
    for j in range(0, KVW_C, LANES):
        r = rope(u[:, j:j + LANES])
        kf_o[:, j:j + LANES] = r
        k_o[:, j:j + LANES] = r.astype(BF16)
    u = jnp.dot(xn, w_ref[:, W_C + KVW_C:W_C + 2 * KVW_C], preferred_element_type=F32)
    vf_o[...] = u
    v_o[...] = u.astype(BF16)
    g0 = W_C + 2 * KVW_C
    for c0 in range(0, W_C, blk):
        u = jnp.dot(xn, w_ref[:, g0 + c0:g0 + c0 + blk], preferred_element_type=F32)
        sg_o[:, c0:c0 + blk] = _silu(u).astype(BF16)


def _proj_c(x2d, g_pre, w_bf16, cos, sin, tm):
    T = x2d.shape[0]
    nt = cos.shape[0] // tm
    row = lambda w: pl.BlockSpec((tm, w), lambda i: (i, 0))
    full = lambda a: pl.BlockSpec(a.shape, lambda i: (0, 0))
    tab = pl.BlockSpec((tm, LANES), lambda i: (i % nt, 0))
    sd = jax.ShapeDtypeStruct
    return pl.pallas_call(
        _proj_c_kernel,
        grid=(T // tm,),
        in_specs=[row(D_MODEL), full(g_pre), full(w_bf16), tab, tab],
        out_specs=[row(W_C), row(KVW_C), row(KVW_C), row(W_C), row(KVW_C), row(KVW_C)],
        out_shape=[sd((T, W_C), BF16), sd((T, KVW_C), BF16), sd((T, KVW_C), BF16), sd((T, W_C), BF16),
                   sd((T, KVW_C), F32), sd((T, KVW_C), F32)],
        compiler_params=_params(1),
        name="proj_c",
    )(x2d, g_pre, w_bf16, cos, sin)


def _attn_c_step(q_ref, k_ref, v_ref, sg_ref, sink_ref, o_ref, q0, k0, width, mcol0, n_valid):
    kw = k_ref[pl.ds(k0, width), :]
    vw = v_ref[pl.ds(k0, width), :]
    i = lax.broadcasted_iota(jnp.int32, (QROWS, width), 0) // CHUNK
    j = lax.broadcasted_iota(jnp.int32, (QROWS, width), 1)
    jc = (j + mcol0) // CHUNK
    valid = (jc <= i + WINDOW_C // CHUNK) & (jc >= i)
    if n_valid is not None:
        valid = valid & (j < n_valid)
    low = _low_half((QROWS, LANES))
    pair = pl.program_id(0)
    for g in range(G_C):
        cols = slice(g * LANES, (g + 1) * LANES)
        qg = q_ref[pl.ds(q0, QROWS), cols]
        outs = []
        for hh in range(2):
            s = jnp.where(valid, _dot_nt(_head_of_pair(qg, low, hh), kw), NEG_INF)
            sk = sink_ref[(2 * pair + hh) * G_C + g]
            mx = jnp.maximum(jnp.max(s, axis=-1, keepdims=True), sk)
            p = jnp.exp(s - mx)
            den = jnp.sum(p, axis=-1, keepdims=True) + jnp.exp(sk - mx)
            outs.append(jnp.dot(p.astype(BF16), vw, preferred_element_type=F32) / den)
        o = jnp.where(low, outs[0], outs[1])
        o_ref[pl.ds(q0, QROWS), cols] = (o * sg_ref[pl.ds(q0, QROWS), cols].astype(F32)).astype(BF16)


def _attn_c_kernel(sink_ref, q_ref, k_ref, v_ref, sg_ref, o_ref, *, n_steps):
    refs = (q_ref, k_ref, v_ref, sg_ref, sink_ref, o_ref)
    _attn_c_step(*refs, 0, 0, QROWS, C_WIN - QROWS, None)

    def body(m, c):
        _attn_c_step(*refs, pl.multiple_of(m * QROWS, QROWS),
                     pl.multiple_of((m - 1) * QROWS, QROWS), C_WIN, 0, None)
        return c

    lax.fori_loop(1, n_steps, body, 0)


def _attn_c_window_kernel(sink_ref, q_ref, k_ref, v_ref, sg_ref, o_ref, *, n_valid):
    _attn_c_step(q_ref, k_ref, v_ref, sg_ref, sink_ref, o_ref, 0, 0, C_WIN, 0, n_valid)


def _attn_c(q, k, v, sg, sinks, n_valid=None):
    B, Sq, _ = q.shape
    Sk = k.shape[1]
    qw = W_C // (KVW_C // LANES)
    qspec = pl.BlockSpec((None, Sq, qw), lambda p, b: (b, 0, p))
    kspec = pl.BlockSpec((None, Sk, LANES), lambda p, b: (b, 0, p))
    sspec = pl.BlockSpec(memory_space=pltpu.SMEM)
    if n_valid is None:
        kern = functools.partial(_attn_c_kernel, n_steps=Sq // QROWS)
    else:
        kern = functools.partial(_attn_c_window_kernel, n_valid=n_valid)
    return pl.pallas_call(
        kern,
        grid=(KVW_C // LANES, B),
        in_specs=[sspec, qspec, kspec, kspec, qspec],
        out_specs=qspec,
        out_shape=jax.ShapeDtypeStruct(q.shape, BF16),
        compiler_params=_params(2),
        name="attn_c" if n_valid is None else "attn_c_window",
    )(sinks, q, k, v, sg)


def _c_head_order():
    heads = [(2 * P + hh) * G_C + g for P in range(KV_C // 2) for g in range(G_C) for hh in range(2)]
    return jnp.asarray([h * HEAD_DIM + d for h in heads for d in range(HEAD_DIM)], jnp.int32)


def _pad_rows(x, n):
    return jnp.pad(x, ((0, 0), (0, n - x.shape[1]), (0, 0)))


def _heads(x2d, B, h):
    return x2d.reshape(B, x2d.shape[0] // B, h, HEAD_DIM)


def _ab_layer(x, cache, g_pre, w_in, w_out, g_post, table):
    B, S, _ = x.shape
    T = B * S
    tm = min(T, 512)
    x2d = x.reshape(T, D_MODEL)
    qa, ka, va, sga, qb, kb, vb, sgb, kaf, vaf, kbf, vbf = _proj_ab(x2d, g_pre[None, :], w_in.astype(BF16), tm)
    r3 = lambda t: t.reshape(B, S, -1)
    if cache is None:
        oa = _attn_a(r3(qa), r3(ka), r3(va), r3(sga), _bias_table(table, A_WIN), window=False)
        ob = _attn_b(r3(qb), r3(kb), r3(vb), r3(sgb))
        la = min(LEFT_CHUNKS_A * CHUNK, S)
        rows = (_heads(kaf, B, H_A)[:, S - la:], _heads(vaf, B, H_A)[:, S - la:],
                _heads(kbf, B, H_B), _heads(vbf, B, H_B))
    else:
        ca_k, ca_v, cb_k, cb_v = cache
        la = ca_k.shape[1]
        assert la == LEFT_CHUNKS_A * CHUNK and S <= CHUNK and cb_k.shape[1] % SB_BLK == 0
        flat = lambda c: c.reshape(B, c.shape[1], -1)
        win_k = _pad_rows(jnp.concatenate([flat(ca_k).astype(BF16), r3(ka)], axis=1), A_WIN)
        win_v = _pad_rows(jnp.concatenate([flat(ca_v).astype(BF16), r3(va)], axis=1), A_WIN)
        oa = _attn_a(_pad_rows(r3(qa), QROWS), win_k, win_v, _pad_rows(r3(sga), QROWS),
                     _bias_table(table, la + S), window=True)[:, :S]
        pb = lambda t: _pad_rows(r3(t), SB_BLK)
        ob = _attn_b_cached(pb(qb), pb(kb), pb(vb), flat(cb_k), flat(cb_v), pb(sgb))[:, :S]
        rows = (_heads(kaf, B, H_A), _heads(vaf, B, H_A), _heads(kbf, B, H_B), _heads(vbf, B, H_B))
    y = _merge([oa.reshape(T, W_A), ob.reshape(T, W_B)], w_out.astype(BF16), g_post[None, :], x2d, tm)
    return y.reshape(B, S, D_MODEL), rows


def _c_layer(x, cache, past, g_pre, w_in, sinks, w_out, g_post):
    B, S, _ = x.shape
    T = B * S
    tm = min(T, 512)
    x2d = x.reshape(T, D_MODEL)
    order = _c_head_order()
    g0 = W_C + 2 * KVW_C
    w_perm = jnp.concatenate([w_in[:, :W_C][:, order], w_in[:, W_C:g0], w_in[:, g0:][:, order]], axis=1)
    if cache is None:
        cos, sin = _rope_tables(S, 0, S)
    else:
        cos, sin = _rope_tables(T, past, S)
    q, k, v, sg, kf, vf = _proj_c(x2d, g_pre[None, :], w_perm.astype(BF16), cos, sin, tm)
    r3 = lambda t: t.reshape(B, S, -1)
    if cache is None:
        o = _attn_c(r3(q), r3(k), r3(v), r3(sg), sinks)
        lc = min(WINDOW_C, S)
        rows = (_heads(kf, B, KV_C)[:, S - lc:], _heads(vf, B, KV_C)[:, S - lc:])
    else:
        cc_k, cc_v = cache
        lc = cc_k.shape[1]
        assert lc == WINDOW_C and S <= CHUNK
        flat = lambda c: c.reshape(B, lc, -1).astype(BF16)
        win_k = _pad_rows(jnp.concatenate([flat(cc_k), r3(k)], axis=1), C_WIN)
        win_v = _pad_rows(jnp.concatenate([flat(cc_v), r3(v)], axis=1), C_WIN)
        o = _attn_c(_pad_rows(r3(q), QROWS), win_k, win_v, _pad_rows(r3(sg), QROWS), sinks,
                    n_valid=lc + S)[:, :S]
        rows = (_heads(kf, B, KV_C), _heads(vf, B, KV_C))
    y = _merge([o.reshape(T, W_C)], w_out[order, :].astype(BF16), g_post[None, :], x2d, tm)
    return y.reshape(B, S, D_MODEL), rows


def kernel(x_prompt, x_sample, cache_a_k, cache_a_v, cache_b_k, cache_b_v, cache_c_k, cache_c_v,
           ab_norm_pre, ab_w_in, ab_w_out, ab_norm_post, a_rel_bias,
           c_norm_pre, c_w_in, c_sinks, c_w_out, c_norm_post):
    past = cache_b_k.shape[2]
    depth = ab_w_in.shape[0] + c_w_in.shape[0]
    yp, ys = x_prompt, x_sample
    ab_p, ab_s, c_p, c_s = [], [], [], []
    for layer in range(depth):
        i = layer // 2
        if layer % 2 == 0:
            w = (ab_norm_pre[i], ab_w_in[i], ab_w_out[i], ab_norm_post[i], a_rel_bias[i])
            yp, rows = _ab_layer(yp, None, *w)
            ab_p.append(rows)
            ys, rows = _ab_layer(ys, (cache_a_k[i], cache_a_v[i], cache_b_k[i], cache_b_v[i]), *w)
            ab_s.append(rows)
        else:
            w = (c_norm_pre[i], c_w_in[i], c_sinks[i], c_w_out[i], c_norm_post[i])
            yp, rows = _c_layer(yp, None, past, *w)
            c_p.append(rows)
            ys, rows = _c_layer(ys, (cache_c_k[i], cache_c_v[i]), past, *w)
            c_s.append(rows)
    st = lambda rows, j: jnp.stack([r[j] for r in rows])
    return (yp, ys,
            st(ab_p, 0), st(ab_p, 1), st(ab_p, 2), st(ab_p, 3), st(c_p, 0), st(c_p, 1),
            st(ab_s, 0), st(ab_s, 1), st(ab_s, 2), st(ab_s, 3), st(c_s, 0), st(c_s, 1))
```

```python
import functools

import jax
import jax.numpy as jnp
from jax import lax
from jax.experimental import pallas as pl
from jax.experimental.pallas import tpu as pltpu

D_MODEL = 1024
CHUNK = 64
HEAD_DIM = 64
H_A = 8
H_B = 8
LEFT_CHUNKS_A = 8
REL_CLIP_A = 128
N_REL_A = 2 * REL_CLIP_A + 1
W_A = H_A * HEAD_DIM
W_B = H_B * HEAD_DIM
H_C = 16
KV_C = 4
G_C = H_C // KV_C
WINDOW_C = 128
W_C = H_C * HEAD_DIM
KVW_C = KV_C * HEAD_DIM
ROPE_THETA = 10000.0
RMS_EPS = 1e-6
NEG_INF = -1e30
SCALE = HEAD_DIM ** -0.5

LANES = 128
QROWS = 128
A_WIN = (LEFT_CHUNKS_A + 2) * CHUNK
A_TAB = 384
A_ROLLW = A_WIN + QROWS
C_WIN = WINDOW_C + QROWS
SB_BLK = 256
SB_DEAD = 104.0
VMEM_LIMIT = 48 * 1024 * 1024

F32 = jnp.float32
BF16 = jnp.bfloat16


def _params(n_axes, semantics="parallel"):
    return pltpu.CompilerParams(dimension_semantics=(semantics,) * n_axes,
                                vmem_limit_bytes=VMEM_LIMIT)


def _rms_scale(x):
    return lax.rsqrt(jnp.mean(x * x, axis=-1, keepdims=True) + RMS_EPS)


def _silu(u):
    return u / (1.0 + jnp.exp(-u))


def _low_half(shape):
    return lax.broadcasted_iota(jnp.int32, shape, 1) < HEAD_DIM


def _head_of_pair(x_bf16, low, h):
    xf = x_bf16.astype(F32)
    return jnp.where(low if h == 0 else jnp.logical_not(low), xf, 0.0).astype(BF16)


def _dot_nt(a, b):
    return lax.dot_general(a, b, (((1,), (1,)), ((), ())), preferred_element_type=F32)


def _store_heads(o_ref, u, n_heads):
    for h in range(n_heads):
        o_ref[pl.ds(h, u.shape[0], stride=n_heads), :] = u[:, h * HEAD_DIM:(h + 1) * HEAD_DIM]


def _proj_ab_kernel(x_ref, g_ref, w_ref, qa, ka, va, sga, qb, kb, vb, sgb, kaf, vaf, kbf, vbf,
                    *, tiles_per_seq, keep_a):
    x = x_ref[...]
    xn = (x * _rms_scale(x) * g_ref[...]).astype(BF16)
    tm = x.shape[0]

    def col(j):
        return jnp.dot(xn, w_ref[:, j * W_A:(j + 1) * W_A], preferred_element_type=F32)

    def keep_tail(o_ref, u):
        if tiles_per_seq == 1:
            _store_heads(o_ref, u[tm - keep_a:, :], H_A)
        else:
            @pl.when(pl.program_id(0) % tiles_per_seq == tiles_per_seq - 1)
            def _():
                _store_heads(o_ref, u[tm - keep_a:, :], H_A)

    qa[...] = (col(0) * SCALE).astype(BF16)
    u = col(1); ka[...] = u.astype(BF16); keep_tail(kaf, u)
    u = col(2); va[...] = u.astype(BF16); keep_tail(vaf, u)
    sga[...] = _silu(col(3)).astype(BF16)
    qb[...] = (col(4) * SCALE).astype(BF16)
    u = col(5); kb[...] = u.astype(BF16); _store_heads(kbf, u, H_B)
    u = col(6); vb[...] = u.astype(BF16); _store_heads(vbf, u, H_B)
    sgb[...] = _silu(col(7)).astype(BF16)


def _proj_ab(x2d, g_pre, w_bf16, tm, tiles_per_seq, keep_a):
    T = x2d.shape[0]
    n_tiles = T // tm
    row = lambda w: pl.BlockSpec((tm, w), lambda i: (i, 0))
    const = lambda a: pl.BlockSpec(a.shape, lambda i: (0, 0), pipeline_mode=pl.Buffered(1))
    bshape = jax.ShapeDtypeStruct((T, W_A), BF16)
    a_spec = pl.BlockSpec((keep_a * H_A, HEAD_DIM), lambda i: (i // tiles_per_seq, 0))
    a_shape = jax.ShapeDtypeStruct((n_tiles // tiles_per_seq * keep_a * H_A, HEAD_DIM), F32)
    b_spec = pl.BlockSpec((tm * H_B, HEAD_DIM), lambda i: (i, 0))
    b_shape = jax.ShapeDtypeStruct((T * H_B, HEAD_DIM), F32)
    return pl.pallas_call(
        functools.partial(_proj_ab_kernel, tiles_per_seq=tiles_per_seq, keep_a=keep_a),
        grid=(n_tiles,),
        in_specs=[row(D_MODEL), const(g_pre), const(w_bf16)],
        out_specs=[row(W_A)] * 8 + [a_spec, a_spec, b_spec, b_spec],
        out_shape=[bshape] * 8 + [a_shape, a_shape, b_shape, b_shape],
        compiler_params=_params(1, "arbitrary"),
        name="proj_ab",
    )(x2d, g_pre, w_bf16)


def _merge_kernel(*refs, n_parts):
    h_refs = refs[:n_parts]
    w_ref, g_ref, x_ref, y_ref = refs[n_parts:]
    acc = None
    off = 0
    for h in h_refs:
        k = h.shape[1]
        d = jnp.dot(h[...], w_ref[off:off + k, :], preferred_element_type=F32)
        acc = d if acc is None else acc + d
        off += k
    y_ref[...] = x_ref[...] + acc * _rms_scale(acc) * g_ref[...]


def _merge(parts, w_bf16, g_post, x2d, tm):
    T = x2d.shape[0]
    row = lambda w: pl.BlockSpec((tm, w), lambda i: (i, 0))
    full = lambda a: pl.BlockSpec(a.shape, lambda i: (0, 0))
    return pl.pallas_call(
        functools.partial(_merge_kernel, n_parts=len(parts)),
        grid=(T // tm,),
        in_specs=[row(p.shape[1]) for p in parts] + [full(w_bf16), full(g_post), row(D_MODEL)],
        out_specs=row(D_MODEL),
        out_shape=jax.ShapeDtypeStruct((T, D_MODEL), F32),
        compiler_params=_params(1),
        name="merge",
    )(*parts, w_bf16, g_post, x2d)


def _bias_kernel(tab_ref, out_ref, *, n_valid):
    m = lax.broadcasted_iota(jnp.int32, (A_TAB, A_ROLLW), 1)
    t = lax.broadcasted_iota(jnp.int32, (A_TAB, A_ROLLW), 0)
    idx = jnp.where(m <= A_WIN, jnp.clip(A_WIN - QROWS - m, -REL_CLIP_A, REL_CLIP_A) + REL_CLIP_A,
                    2 * REL_CLIP_A)
    onehot = jnp.where(t == idx, 1.0, 0.0).astype(BF16)
    tab = tab_ref[...]
    hi = tab.astype(BF16)
    r1 = tab - hi.astype(F32)
    mid = r1.astype(BF16)
    lo = (r1 - mid.astype(F32)).astype(BF16)
    src = (jnp.dot(hi, onehot, preferred_element_type=F32)
           + jnp.dot(mid, onehot, preferred_element_type=F32)
           + jnp.dot(lo, onehot, preferred_element_type=F32))
    i = lax.broadcasted_iota(jnp.int32, (QROWS, A_WIN), 0)
    j = lax.broadcasted_iota(jnp.int32, (QROWS, A_WIN), 1)
    qc = LEFT_CHUNKS_A + i // CHUNK
    kc = j // CHUNK
    valid = (kc <= qc) & (kc >= qc - LEFT_CHUNKS_A) & (j < n_valid)
    for h in range(H_A):
        row = jnp.broadcast_to(src[h:h + 1, :], (QROWS, A_ROLLW))
        toep = pltpu.roll(row, 0, 1, stride=1, stride_axis=0)
        out_ref[h] = jnp.where(valid, toep[:, :A_WIN], NEG_INF)


def _bias_table(table, n_valid):
    tab = jnp.pad(table, ((0, 0), (0, A_TAB - N_REL_A)))
    return pl.pallas_call(
        functools.partial(_bias_kernel, n_valid=n_valid),
        out_shape=jax.ShapeDtypeStruct((H_A, QROWS, A_WIN), F32),
        name="a_bias",
    )(tab)


def _attn_a_step(q_ref, k_ref, v_ref, sg_ref, bias_ref, o_ref, q0, k0, width, bcol0):
    qm = q_ref[pl.ds(q0, QROWS), :]
    kw = k_ref[pl.ds(k0, width), :]
    vw = v_ref[pl.ds(k0, width), :]
    low = _low_half((QROWS, LANES))
    outs = []
    for h in range(2):
        s = _dot_nt(_head_of_pair(qm, low, h), kw) + bias_ref[h, :, bcol0:bcol0 + width]
        p = jnp.exp(s - jnp.max(s, axis=-1, keepdims=True))
        den = jnp.sum(p, axis=-1, keepdims=True)
        outs.append(jnp.dot(p.astype(BF16), vw, preferred_element_type=F32) / den)
    o = jnp.where(low, outs[0], outs[1])
    o_ref[pl.ds(q0, QROWS), :] = (o * sg_ref[pl.ds(q0, QROWS), :].astype(F32)).astype(BF16)


def _attn_a_kernel(q_ref, k_ref, v_ref, sg_ref, bias_ref, o_ref, *, n_steps):
    lead = A_WIN // QROWS - 1
    refs = (q_ref, k_ref, v_ref, sg_ref, bias_ref, o_ref)
    for m in range(min(lead, n_steps)):
        _attn_a_step(*refs, m * QROWS, 0, (m + 1) * QROWS, (lead - m) * QROWS)

    def body(m, c):
        _attn_a_step(*refs, pl.multiple_of(m * QROWS, QROWS),
                     pl.multiple_of((m - lead) * QROWS, QROWS), A_WIN, 0)
        return c

    if n_steps > lead:
        lax.fori_loop(lead, n_steps, body, 0, unroll=2)


def _attn_a_window_kernel(q_ref, k_ref, v_ref, sg_ref, bias_ref, o_ref):
    _attn_a_step(q_ref, k_ref, v_ref, sg_ref, bias_ref, o_ref, 0, 0, A_WIN, 0)


def _attn_a(q, k, v, sg, bias, window):
    B, Sq, _ = q.shape
    Sk = k.shape[1]
    qspec = pl.BlockSpec((None, Sq, LANES), lambda p, b: (b, 0, p))
    kspec = pl.BlockSpec((None, Sk, LANES), lambda p, b: (b, 0, p))
    bspec = pl.BlockSpec((2, QROWS, A_WIN), lambda p, b: (p, 0, 0))
    kern = _attn_a_window_kernel if window else functools.partial(_attn_a_kernel, n_steps=Sq // QROWS)
    return pl.pallas_call(
        kern,
        grid=(W_A // LANES, B),
        in_specs=[qspec, kspec, kspec, qspec, bspec],
        out_specs=qspec,
        out_shape=jax.ShapeDtypeStruct(q.shape, BF16),
        compiler_params=_params(2),
        name="attn_a_window" if window else "attn_a",
    )(q, k, v, sg, bias)


def _sb_block(qh, kj, vj, state, tri2, causal):
    decay, acc = state
    z = _dot_nt(qh, kj)
    sp = jnp.maximum(z, 0.0) + jnp.log(1.0 + jnp.exp(-jnp.abs(z)))
    spm = sp if causal is None else jnp.where(causal, sp, 0.0)
    hi = spm.astype(BF16)
    lo = (spm - hi.astype(F32)).astype(BF16)
    inner = jnp.dot(jnp.concatenate([hi, lo], axis=1), tri2, preferred_element_type=F32)
    a = jnp.exp((z - sp) - (inner + decay))
    if causal is not None:
        a = jnp.where(causal, a, 0.0)
    acc = acc + jnp.dot(a.astype(BF16), vj, preferred_element_type=F32)
    decay = decay + jnp.sum(spm, axis=-1, keepdims=True)
    return decay, acc


def _sb_consts():
    r = lax.broadcasted_iota(jnp.int32, (2 * SB_BLK, SB_BLK), 0) % SB_BLK
    c = lax.broadcasted_iota(jnp.int32, (2 * SB_BLK, SB_BLK), 1)
    tri2 = jnp.where(r > c, 1.0, 0.0).astype(BF16)
    rr = lax.broadcasted_iota(jnp.int32, (SB_BLK, SB_BLK), 0)
    cc = lax.broadcasted_iota(jnp.int32, (SB_BLK, SB_BLK), 1)
    return tri2, cc < rr


def _sb_query_block(qm, kd, vd, kp_ref, vp_ref, n_past, tri2, causal, low):
    qs = [_head_of_pair(qm, low, h) for h in range(2)]
    zero = (jnp.zeros((SB_BLK, 1), F32), jnp.zeros((SB_BLK, LANES), F32))
    st = tuple(_sb_block(qs[h], kd, vd, zero, tri2, causal) for h in range(2))
    if not (isinstance(n_past, int) and n_past == 0):

        def past(j, st):
            r0 = j * SB_BLK if isinstance(j, int) else pl.multiple_of(j * SB_BLK, SB_BLK)
            kj = kp_ref[pl.ds(r0, SB_BLK), :].astype(BF16)
            vj = vp_ref[pl.ds(r0, SB_BLK), :].astype(BF16)
            return tuple(_sb_block(qs[h], kj, vj, st[h], tri2, None) for h in range(2))

        def live(st):
            return (jnp.min(jnp.minimum(st[0][0], st[1][0])) < SB_DEAD).astype(jnp.int32)

        def body(c):
            s = past(c[0], c[2])
            return c[0] - 1, live(s), s

        st = past(n_past - 1, st)
        _, _, st = lax.while_loop(lambda c: (c[0] >= 0) & (c[1] > 0), body,
                                  (jnp.asarray(n_past - 2, jnp.int32), live(st), st))
    return jnp.where(low, st[0][1], st[1][1])


def _attn_b_kernel(q_ref, k_ref, v_ref, sg_ref, o_ref):
    tri, causal = _sb_consts()
    low = _low_half((SB_BLK, LANES))

    def qblock(i, c):
        rows = pl.ds(i * SB_BLK if isinstance(i, int) else pl.multiple_of(i * SB_BLK, SB_BLK), SB_BLK)
        o = _sb_query_block(q_ref[rows, :], k_ref[rows, :], v_ref[rows, :], k_ref, v_ref, i,
                            tri, causal, low)
        o_ref[rows, :] = (o * sg_ref[rows, :].astype(F32)).astype(BF16)
        return c

    qblock(0, 0)
    lax.fori_loop(1, q_ref.shape[0] // SB_BLK, qblock, 0)


def _attn_b_cached_kernel(q_ref, kd_ref, vd_ref, kp_ref, vp_ref, sg_ref, o_ref):
    tri, causal = _sb_consts()
    low = _low_half((SB_BLK, LANES))
    o = _sb_query_block(q_ref[...], kd_ref[...], vd_ref[...], kp_ref, vp_ref,
                        kp_ref.shape[0] // SB_BLK, tri, causal, low)
    o_ref[...] = (o * sg_ref[...].astype(F32)).astype(BF16)


def _attn_b(q, k, v, sg):
    B, S, _ = q.shape
    spec = pl.BlockSpec((None, S, LANES), lambda p, b: (b, 0, p))
    return pl.pallas_call(
        _attn_b_kernel,
        grid=(W_B // LANES, B),
        in_specs=[spec] * 4,
        out_specs=spec,
        out_shape=jax.ShapeDtypeStruct(q.shape, BF16),
        compiler_params=_params(2),
        name="attn_b",
    )(q, k, v, sg)


def _attn_b_cached(q, kd, vd, kp, vp, sg):
    B = q.shape[0]
    P = kp.shape[1]
    spec = pl.BlockSpec((None, SB_BLK, LANES), lambda p, b: (b, 0, p))
    pspec = pl.BlockSpec((None, P, LANES), lambda p, b: (b, 0, p))
    return pl.pallas_call(
        _attn_b_cached_kernel,
        grid=(W_B // LANES, B),
        in_specs=[spec, spec, spec, pspec, pspec, spec],
        out_specs=spec,
        out_shape=jax.ShapeDtypeStruct(q.shape, BF16),
        compiler_params=_params(2),
        name="attn_b_cached",
    )(q, kd, vd, kp, vp, sg)


def _rope_table_kernel(inv_ref, cos_ref, sin_ref, *, pos0, period):
    shape = cos_ref.shape
    row = lax.broadcasted_iota(jnp.int32, shape, 0)
    lane = lax.broadcasted_iota(jnp.int32, shape, 1)
    ang = (pos0 + row % period).astype(F32) * inv_ref[...]
    cos_ref[...] = jnp.cos(ang)
    s = jnp.sin(ang)
    sin_ref[...] = jnp.where(lane % HEAD_DIM < HEAD_DIM // 2, -s, s)


def _rope_tables(n_rows, pos0, period):
    half = HEAD_DIM // 2
    inv = ROPE_THETA ** (-jnp.arange(half, dtype=F32) * (2.0 / HEAD_DIM))
    inv = jnp.tile(inv, LANES // half)[None, :]
    shape = jax.ShapeDtypeStruct((n_rows, LANES), F32)
    return pl.pallas_call(
        functools.partial(_rope_table_kernel, pos0=pos0, period=period),
        out_shape=[shape, shape],
        name="rope_table",
    )(inv)


def _proj_c_kernel(x_ref, g_ref, w_ref, cos_ref, sin_ref, q_o, k_o, v_o, sg_o, kf_o, vf_o,
                   *, tiles_per_seq):
    x = x_ref[...]
    xn = (x * _rms_scale(x) * g_ref[...]).astype(BF16)
    tm = x.shape[0]
    keep = kf_o.shape[0]

    def keep_tail(o_ref, u):
        if tiles_per_seq == 1:
            o_ref[...] = u[tm - keep:, :]
        else:
            @pl.when(pl.program_id(0) % tiles_per_seq == tiles_per_seq - 1)
            def _():
                o_ref[...] = u[tm - keep:, :]
    cos = cos_ref[...]
    sin = sin_ref[...]
    first = lax.broadcasted_iota(jnp.int32, cos.shape, 1) % HEAD_DIM < HEAD_DIM // 2

    def rope(u):
        partner = jnp.where(first, pltpu.roll(u, LANES - HEAD_DIM // 2, 1), pltpu.roll(u, HEAD_DIM // 2, 1))
        return u * cos + partner * sin

    blk = 4 * LANES
    for c0 in range(0, W_C, blk):
        u = jnp.dot(xn, w_ref[:, c0:c0 + blk], preferred_element_type=F32)
        for j in range(0, blk, LANES):
            q_o[:, c0 + j:c0 + j + LANES] = (rope(u[:, j:j + LANES]) * SCALE).astype(BF16)
    u = jnp.dot(xn, w_ref[:, W_C:W_C + KVW_C], preferred_element_type=F32)
    r = jnp.concatenate([rope(u[:, j:j + LANES]) for j in range(0, KVW_C, LANES)], axis=1)
    k_o[...] = r.astype(BF16)
    keep_tail(kf_o, r)
    u = jnp.dot(xn, w_ref[:, W_C + KVW_C:W_C + 2 * KVW_C], preferred_element_type=F32)
    v_o[...] = u.astype(BF16)
    keep_tail(vf_o, u)
    g0 = W_C + 2 * KVW_C
    for c0 in range(0, W_C, blk):
        u = jnp.dot(xn, w_ref[:, g0 + c0:g0 + c0 + blk], preferred_element_type=F32)
        sg_o[:, c0:c0 + blk] = _silu(u).astype(BF16)


def _proj_c(x2d, g_pre, w_bf16, cos, sin, tm, tiles_per_seq, keep):
    T = x2d.shape[0]
    n_tiles = T // tm
    nt = cos.shape[0] // tm
    row = lambda w: pl.BlockSpec((tm, w), lambda i: (i, 0))
    const = lambda a: pl.BlockSpec(a.shape, lambda i: (0, 0), pipeline_mode=pl.Buffered(1))
    tab = pl.BlockSpec((tm, LANES), lambda i: (i % nt, 0))
    tail = pl.BlockSpec((keep, KVW_C), lambda i: (i // tiles_per_seq, 0))
    sd = jax.ShapeDtypeStruct
    tail_shape = sd((n_tiles // tiles_per_seq * keep, KVW_C), F32)
    return pl.pallas_call(
        functools.partial(_proj_c_kernel, tiles_per_seq=tiles_per_seq),
        grid=(n_tiles,),
        in_specs=[row(D_MODEL), const(g_pre), const(w_bf16), tab, tab],
        out_specs=[row(W_C), row(KVW_C), row(KVW_C), row(W_C), tail, tail],
        out_shape=[sd((T, W_C), BF16), sd((T, KVW_C), BF16), sd((T, KVW_C), BF16), sd((T, W_C), BF16),
                   tail_shape, tail_shape],
        compiler_params=_params(1, "arbitrary"),
        name="proj_c",
    )(x2d, g_pre, w_bf16, cos, sin)


def _attn_c_step(q_ref, k_ref, v_ref, sg_ref, sink_ref, o_ref, q0, k0, width, mcol0, n_valid):
    kw = k_ref[pl.ds(k0, width), :]
    vw = v_ref[pl.ds(k0, width), :]
    i = lax.broadcasted_iota(jnp.int32, (QROWS, width), 0) // CHUNK
    j = lax.broadcasted_iota(jnp.int32, (QROWS, width), 1)
    jc = (j + mcol0) // CHUNK
    valid = (jc <= i + WINDOW_C // CHUNK) & (jc >= i)
    if n_valid is not None:
        valid = valid & (j < n_valid)
    low = _low_half((QROWS, LANES))
    pair = pl.program_id(0)
    for g in range(G_C):
        cols = slice(g * LANES, (g + 1) * LANES)
        qg = q_ref[pl.ds(q0, QROWS), cols]
        outs = []
        for hh in range(2):
            s = jnp.where(valid, _dot_nt(_head_of_pair(qg, low, hh), kw), NEG_INF)
            sk = sink_ref[(2 * pair + hh) * G_C + g]
            mx = jnp.maximum(jnp.max(s, axis=-1, keepdims=True), sk)
            p = jnp.exp(s - mx)
            den = jnp.sum(p, axis=-1, keepdims=True) + jnp.exp(sk - mx)
            outs.append(jnp.dot(p.astype(BF16), vw, preferred_element_type=F32) / den)
        o = jnp.where(low, outs[0], outs[1])
        o_ref[pl.ds(q0, QROWS), cols] = (o * sg_ref[pl.ds(q0, QROWS), cols].astype(F32)).astype(BF16)


def _attn_c_kernel(sink_ref, q_ref, k_ref, v_ref, sg_ref, o_ref, *, n_steps):
    refs = (q_ref, k_ref, v_ref, sg_ref, sink_ref, o_ref)
    _attn_c_step(*refs, 0, 0, QROWS, C_WIN - QROWS, None)

    def body(m, c):
        _attn_c_step(*refs, pl.multiple_of(m * QROWS, QROWS),
                     pl.multiple_of((m - 1) * QROWS, QROWS), C_WIN, 0, None)
        return c

    lax.fori_loop(1, n_steps, body, 0, unroll=3)


def _attn_c_window_kernel(sink_ref, q_ref, k_ref, v_ref, sg_ref, o_ref, *, n_valid):
    _attn_c_step(q_ref, k_ref, v_ref, sg_ref, sink_ref, o_ref, 0, 0, C_WIN, 0, n_valid)


def _attn_c(q, k, v, sg, sinks, n_valid=None):
    B, Sq, _ = q.shape
    Sk = k.shape[1]
    qw = W_C // (KVW_C // LANES)
    qspec = pl.BlockSpec((None, Sq, qw), lambda p, b: (b, 0, p))
    kspec = pl.BlockSpec((None, Sk, LANES), lambda p, b: (b, 0, p))
    sspec = pl.BlockSpec(memory_space=pltpu.SMEM)
    if n_valid is None:
        kern = functools.partial(_attn_c_kernel, n_steps=Sq // QROWS)
    else:
        kern = functools.partial(_attn_c_window_kernel, n_valid=n_valid)
    return pl.pallas_call(
        kern,
        grid=(KVW_C // LANES, B),
        in_specs=[sspec, qspec, kspec, kspec, qspec],
        out_specs=qspec,
        out_shape=jax.ShapeDtypeStruct(q.shape, BF16),
        compiler_params=_params(2),
        name="attn_c" if n_valid is None else "attn_c_window",
    )(sinks, q, k, v, sg)


def _c_head_order():
    heads = [(2 * P + hh) * G_C + g for P in range(KV_C // 2) for g in range(G_C) for hh in range(2)]
    return jnp.asarray([h * HEAD_DIM + d for h in heads for d in range(HEAD_DIM)], jnp.int32)


def _pad_rows(x, n):
    return jnp.pad(x, ((0, 0), (0, n - x.shape[1]), (0, 0)))


def _heads(x2d, B, h):
    return x2d.reshape(B, x2d.shape[0] // B, h, HEAD_DIM)


def _ab_layer(x, cache, g_pre, w_in, w_out, g_post, table):
    B, S, _ = x.shape
    T = B * S
    x2d = x.reshape(T, D_MODEL)
    r3 = lambda t: t.reshape(B, S, -1)
    cached = lambda t, h: t.reshape(B, -1, h, HEAD_DIM)
    if cache is None:
        tm = min(S, 512)
        la = min(LEFT_CHUNKS_A * CHUNK, S)
        assert S % tm == 0 and la <= tm
        proj = _proj_ab(x2d, g_pre[None, :], w_in.astype(BF16), tm, S // tm, la)
        qa, ka, va, sga, qb, kb, vb, sgb = proj[:8]
        oa = _attn_a(r3(qa), r3(ka), r3(va), r3(sga), _bias_table(table, A_WIN), window=False)
        ob = _attn_b(r3(qb), r3(kb), r3(vb), r3(sgb))
    else:
        ca_k, ca_v, cb_k, cb_v = cache
        la = ca_k.shape[1]
        assert la == LEFT_CHUNKS_A * CHUNK and S <= CHUNK and cb_k.shape[1] % SB_BLK == 0
        tm = T
        proj = _proj_ab(x2d, g_pre[None, :], w_in.astype(BF16), tm, 1, tm)
        qa, ka, va, sga, qb, kb, vb, sgb = proj[:8]
        flat = lambda c: c.reshape(B, c.shape[1], -1)
        win_k = _pad_rows(jnp.concatenate([flat(ca_k).astype(BF16), r3(ka)], axis=1), A_WIN)
        win_v = _pad_rows(jnp.concatenate([flat(ca_v).astype(BF16), r3(va)], axis=1), A_WIN)
        oa = _attn_a(_pad_rows(r3(qa), QROWS), win_k, win_v, _pad_rows(r3(sga), QROWS),
                     _bias_table(table, la + S), window=True)[:, :S]
        pb = lambda t: _pad_rows(r3(t), SB_BLK)
        ob = _attn_b_cached(pb(qb), pb(kb), pb(vb), flat(cb_k), flat(cb_v), pb(sgb))[:, :S]
    rows = (cached(proj[8], H_A), cached(proj[9], H_A), cached(proj[10], H_B), cached(proj[11], H_B))
    y = _merge([oa.reshape(T, W_A), ob.reshape(T, W_B)], w_out.astype(BF16), g_post[None, :], x2d, tm)
    return y.reshape(B, S, D_MODEL), rows


def _c_layer(x, cache, past, g_pre, w_in, sinks, w_out, g_post):
    B, S, _ = x.shape
    T = B * S
    x2d = x.reshape(T, D_MODEL)
    order = _c_head_order()
    g0 = W_C + 2 * KVW_C
    w_perm = jnp.concatenate([w_in[:, :W_C][:, order], w_in[:, W_C:g0], w_in[:, g0:][:, order]], axis=1)
    r3 = lambda t: t.reshape(B, S, -1)
    if cache is None:
        tm = min(S, 512)
        lc = min(WINDOW_C, S)
        assert S % tm == 0 and lc <= tm
        cos, sin = _rope_tables(S, 0, S)
        q, k, v, sg, kf, vf = _proj_c(x2d, g_pre[None, :], w_perm.astype(BF16), cos, sin, tm, S // tm, lc)
        o = _attn_c(r3(q), r3(k), r3(v), r3(sg), sinks)
    else:
        tm = T
        cos, sin = _rope_tables(T, past, S)
        q, k, v, sg, kf, vf = _proj_c(x2d, g_pre[None, :], w_perm.astype(BF16), cos, sin, tm, 1, tm)
        cc_k, cc_v = cache
        lc = cc_k.shape[1]
        assert lc == WINDOW_C and S <= CHUNK
        flat = lambda c: c.reshape(B, lc, -1).astype(BF16)
        win_k = _pad_rows(jnp.concatenate([flat(cc_k), r3(k)], axis=1), C_WIN)
        win_v = _pad_rows(jnp.concatenate([flat(cc_v), r3(v)], axis=1), C_WIN)
        o = _attn_c(_pad_rows(r3(q), QROWS), win_k, win_v, _pad_rows(r3(sg), QROWS), sinks,
                    n_valid=lc + S)[:, :S]
    rows = (_heads(kf, B, KV_C), _heads(vf, B, KV_C))
    y = _merge([o.reshape(T, W_C)], w_out[order, :].astype(BF16), g_post[None, :], x2d, tm)
    return y.reshape(B, S, D_MODEL), rows


def kernel(x_prompt, x_sample, cache_a_k, cache_a_v, cache_b_k, cache_b_v, cache_c_k, cache_c_v,
           ab_norm_pre, ab_w_in, ab_w_out, ab_norm_post, a_rel_bias,
           c_norm_pre, c_w_in, c_sinks, c_w_out, c_norm_post):
    past = cache_b_k.shape[2]
    depth = ab_w_in.shape[0] + c_w_in.shape[0]
    yp, ys = x_prompt, x_sample
    ab_p, ab_s, c_p, c_s = [], [], [], []
    for layer in range(depth):
        i = layer // 2
        if layer % 2 == 0:
            w = (ab_norm_pre[i], ab_w_in[i], ab_w_out[i], ab_norm_post[i], a_rel_bias[i])
            yp, rows = _ab_layer(yp, None, *w)
            ab_p.append(rows)
            ys, rows = _ab_layer(ys, (cache_a_k[i], cache_a_v[i], cache_b_k[i], cache_b_v[i]), *w)
            ab_s.append(rows)
        else:
            w = (c_norm_pre[i], c_w_in[i], c_sinks[i], c_w_out[i], c_norm_post[i])
            yp, rows = _c_layer(yp, None, past, *w)
            c_p.append(rows)
            ys, rows = _c_layer(ys, (cache_c_k[i], cache_c_v[i]), past, *w)
            c_s.append(rows)
    st = lambda rows, j: jnp.stack([r[j] for r in rows])
    return (yp, ys,
            st(ab_p, 0), st(ab_p, 1), st(ab_p, 2), st(ab_p, 3), st(c_p, 0), st(c_p, 1),
            st(ab_s, 0), st(ab_s, 1), st(ab_s, 2), st(ab_s, 3), st(c_s, 0), st(c_s, 1))
```

```python
import functools

import jax
import jax.numpy as jnp
from jax import lax
from jax.experimental import pallas as pl
from jax.experimental.pallas import tpu as pltpu

D_MODEL = 1024
CHUNK = 64
HEAD_DIM = 64
H_A = 8
H_B = 8
LEFT_CHUNKS_A = 8
REL_CLIP_A = 128
N_REL_A = 2 * REL_CLIP_A + 1
W_A = H_A * HEAD_DIM
W_B = H_B * HEAD_DIM
H_C = 16
KV_C = 4
G_C = H_C // KV_C
WINDOW_C = 128
W_C = H_C * HEAD_DIM
KVW_C = KV_C * HEAD_DIM
ROPE_THETA = 10000.0
RMS_EPS = 1e-6
NEG_INF = -1e30
SCALE = HEAD_DIM ** -0.5

LANES = 128
QROWS = 128
A_WIN = (LEFT_CHUNKS_A + 2) * CHUNK
A_TAB = 384
A_ROLLW = A_WIN + QROWS
C_WIN = WINDOW_C + QROWS
STRIP = 32
LOG2E = 1.4426950408889634
SB_BLK = 256
SB_DEAD = 104.0 * LOG2E
VMEM_LIMIT = 48 * 1024 * 1024

F32 = jnp.float32
BF16 = jnp.bfloat16


def _params(n_axes, semantics="parallel"):
    return pltpu.CompilerParams(dimension_semantics=(semantics,) * n_axes,
                                vmem_limit_bytes=VMEM_LIMIT)


def _rms_scale(x):
    return lax.rsqrt(jnp.mean(x * x, axis=-1, keepdims=True) + RMS_EPS)


def _silu(u):
    return u / (1.0 + jnp.exp(-u))


def _low_half(shape):
    return lax.broadcasted_iota(jnp.int32, shape, 1) < HEAD_DIM


def _head_of_pair(x_bf16, low, h):
    xf = x_bf16.astype(F32)
    return jnp.where(low if h == 0 else jnp.logical_not(low), xf, 0.0).astype(BF16)


def _dot_nt(a, b):
    return lax.dot_general(a, b, (((1,), (1,)), ((), ())), preferred_element_type=F32)


def _store_heads(o_ref, u, n_heads):
    for h in range(n_heads):
        o_ref[pl.ds(h, u.shape[0], stride=n_heads), :] = u[:, h * HEAD_DIM:(h + 1) * HEAD_DIM]


def _proj_ab_kernel(x_ref, g_ref, w_ref, qa, ka, va, sga, qb, kb, vb, sgb, kaf, vaf, kbf, vbf,
                    *, tiles_per_seq, keep_a):
    x = x_ref[...]
    xn = (x * _rms_scale(x) * g_ref[...]).astype(BF16)
    tm = x.shape[0]

    def col(j):
        return jnp.dot(xn, w_ref[:, j * W_A:(j + 1) * W_A], preferred_element_type=F32)

    qa[...] = (col(0) * (SCALE * LOG2E)).astype(BF16)
    u = col(1); ka[...] = u.astype(BF16); _store_heads(kaf, u[tm - keep_a:, :], H_A)
    u = col(2); va[...] = u.astype(BF16); _store_heads(vaf, u[tm - keep_a:, :], H_A)
    sga[...] = _silu(col(3)).astype(BF16)
    qb[...] = (col(4) * (SCALE * LOG2E)).astype(BF16)
    u = col(5); kb[...] = u.astype(BF16); _store_heads(kbf, u, H_B)
    u = col(6); vb[...] = u.astype(BF16); _store_heads(vbf, u, H_B)
    sgb[...] = _silu(col(7)).astype(BF16)


def _proj_ab(x2d, g_pre, w_bf16, tm, tiles_per_seq, keep_a):
    T = x2d.shape[0]
    n_tiles = T // tm
    row = lambda w: pl.BlockSpec((tm, w), lambda i: (i, 0))
    const = lambda a: pl.BlockSpec(a.shape, lambda i: (0, 0), pipeline_mode=pl.Buffered(1))
    bshape = jax.ShapeDtypeStruct((T, W_A), BF16)
    a_spec = pl.BlockSpec((keep_a * H_A, HEAD_DIM), lambda i: (i // tiles_per_seq, 0))
    a_shape = jax.ShapeDtypeStruct((n_tiles // tiles_per_seq * keep_a * H_A, HEAD_DIM), F32)
    b_spec = pl.BlockSpec((tm * H_B, HEAD_DIM), lambda i: (i, 0))
    b_shape = jax.ShapeDtypeStruct((T * H_B, HEAD_DIM), F32)
    return pl.pallas_call(
        functools.partial(_proj_ab_kernel, tiles_per_seq=tiles_per_seq, keep_a=keep_a),
        grid=(n_tiles,),
        in_specs=[row(D_MODEL), const(g_pre), const(w_bf16)],
        out_specs=[row(W_A)] * 8 + [a_spec, a_spec, b_spec, b_spec],
        out_shape=[bshape] * 8 + [a_shape, a_shape, b_shape, b_shape],
        compiler_params=_params(1, "arbitrary"),
        name="proj_ab",
    )(x2d, g_pre, w_bf16)


def _merge_kernel(*refs, n_parts):
    h_refs = refs[:n_parts]
    w_ref, g_ref, x_ref, y_ref = refs[n_parts:]
    acc = None
    off = 0
    for h in h_refs:
        k = h.shape[1]
        d = jnp.dot(h[...], w_ref[off:off + k, :], preferred_element_type=F32)
        acc = d if acc is None else acc + d
        off += k
    y_ref[...] = x_ref[...] + acc * _rms_scale(acc) * g_ref[...]


def _merge(parts, w_bf16, g_post, x2d, tm):
    T = x2d.shape[0]
    row = lambda w: pl.BlockSpec((tm, w), lambda i: (i, 0))
    full = lambda a: pl.BlockSpec(a.shape, lambda i: (0, 0))
    return pl.pallas_call(
        functools.partial(_merge_kernel, n_parts=len(parts)),
        grid=(T // tm,),
        in_specs=[row(p.shape[1]) for p in parts] + [full(w_bf16), full(g_post), row(D_MODEL)],
        out_specs=row(D_MODEL),
        out_shape=jax.ShapeDtypeStruct((T, D_MODEL), F32),
        compiler_params=_params(1),
        name="merge",
    )(*parts, w_bf16, g_post, x2d)


def _bias_kernel(tab_ref, out_ref, *, n_valid):
    m = lax.broadcasted_iota(jnp.int32, (A_TAB, A_ROLLW), 1)
    t = lax.broadcasted_iota(jnp.int32, (A_TAB, A_ROLLW), 0)
    idx = jnp.where(m <= A_WIN, jnp.clip(A_WIN - QROWS - m, -REL_CLIP_A, REL_CLIP_A) + REL_CLIP_A,
                    2 * REL_CLIP_A)
    onehot = jnp.where(t == idx, 1.0, 0.0).astype(BF16)
    tab = tab_ref[...]
    hi = tab.astype(BF16)
    r1 = tab - hi.astype(F32)
    mid = r1.astype(BF16)
    lo = (r1 - mid.astype(F32)).astype(BF16)
    src = (jnp.dot(hi, onehot, preferred_element_type=F32)
           + jnp.dot(mid, onehot, preferred_element_type=F32)
           + jnp.dot(lo, onehot, preferred_element_type=F32))
    i = lax.broadcasted_iota(jnp.int32, (QROWS, A_WIN), 0)
    j = lax.broadcasted_iota(jnp.int32, (QROWS, A_WIN), 1)
    qc = LEFT_CHUNKS_A + i // CHUNK
    kc = j // CHUNK
    valid = (kc <= qc) & (kc >= qc - LEFT_CHUNKS_A) & (j < n_valid)
    for h in range(H_A):
        row = jnp.broadcast_to(src[h:h + 1, :], (QROWS, A_ROLLW))
        toep = pltpu.roll(row, 0, 1, stride=1, stride_axis=0)
        out_ref[h * QROWS:(h + 1) * QROWS, :] = jnp.where(valid, toep[:, :A_WIN] * LOG2E, NEG_INF)


def _bias_table(table, n_valid):
    tab = jnp.pad(table, ((0, 0), (0, A_TAB - N_REL_A)))
    return pl.pallas_call(
        functools.partial(_bias_kernel, n_valid=n_valid),
        out_shape=jax.ShapeDtypeStruct((H_A * QROWS, A_WIN), F32),
        name="a_bias",
    )(tab)


def _paired_steps(first, n_steps, step):
    pairs = max(n_steps - first, 0) // 2

    def body(t, c):
        step(first + 2 * t, 0)
        step(first + 2 * t + 1, 1)
        return c

    if pairs:
        lax.fori_loop(0, pairs, body, 0)
    for m in range(first + 2 * pairs, n_steps):
        step(m, m % 2)


def _softmax_pv(s_scr, p_scr, v_aug, width):
    rows = s_scr.shape[0]
    for r in range(0, rows, STRIP):
        x = s_scr[r:r + STRIP, :width]
        p_scr[r:r + STRIP, :width] = jnp.exp2(x - jnp.max(x, axis=-1, keepdims=True)).astype(BF16)
    return jnp.dot(p_scr[:, :width], v_aug, preferred_element_type=F32)


def _attn_a_step(q_ref, k_ref, v_ref, sg_ref, bias_ref, o_ref, s_scr, p_scr, q0, k0, width, bcol0):
    qm = q_ref[pl.ds(q0, QROWS), :]
    kw = k_ref[pl.ds(k0, width), :]
    vw = v_ref[pl.ds(k0, width), :]
    low = _low_half((QROWS, LANES))
    q2 = jnp.concatenate([_head_of_pair(qm, low, 0), _head_of_pair(qm, low, 1)], axis=0)
    s_scr[:, :width] = _dot_nt(q2, kw) + bias_ref[:, bcol0:bcol0 + width]
    o2 = _softmax_pv(s_scr, p_scr, jnp.concatenate([vw, jnp.ones_like(vw)], axis=1), width)
    top, bot = o2[:QROWS], o2[QROWS:]
    o = jnp.where(low, top[:, :LANES] / top[:, LANES:], bot[:, :LANES] / bot[:, LANES:])
    o_ref[pl.ds(q0, QROWS), :] = (o * sg_ref[pl.ds(q0, QROWS), :].astype(F32)).astype(BF16)


def _attn_a_kernel(q_ref, k_ref, v_ref, sg_ref, bias_ref, o_ref, s_scr, p_scr, *, n_steps):
    lead = A_WIN // QROWS - 1
    refs = (q_ref, k_ref, v_ref, sg_ref, bias_ref, o_ref)

    def step(m, slot):
        scr = (s_scr.at[slot], p_scr.at[slot])
        if isinstance(m, int) and m < lead:
            _attn_a_step(*refs, *scr, m * QROWS, 0, (m + 1) * QROWS, (lead - m) * QROWS)
        elif isinstance(m, int):
            _attn_a_step(*refs, *scr, m * QROWS, (m - lead) * QROWS, A_WIN, 0)
        else:
            _attn_a_step(*refs, *scr, pl.multiple_of(m * QROWS, QROWS),
                         pl.multiple_of((m - lead) * QROWS, QROWS), A_WIN, 0)

    for m in range(min(lead, n_steps)):
        step(m, m % 2)
    _paired_steps(lead, n_steps, step)


def _attn_a_window_kernel(q_ref, k_ref, v_ref, sg_ref, bias_ref, o_ref, s_scr, p_scr):
    _attn_a_step(q_ref, k_ref, v_ref, sg_ref, bias_ref, o_ref, s_scr.at[0], p_scr.at[0], 0, 0, A_WIN, 0)


def _attn_a(q, k, v, sg, bias, window):
    B, Sq, _ = q.shape
    Sk = k.shape[1]
    qspec = pl.BlockSpec((None, Sq, LANES), lambda p, b: (b, 0, p))
    kspec = pl.BlockSpec((None, Sk, LANES), lambda p, b: (b, 0, p))
    bspec = pl.BlockSpec((2 * QROWS, A_WIN), lambda p, b: (p, 0))
    kern = _attn_a_window_kernel if window else functools.partial(_attn_a_kernel, n_steps=Sq // QROWS)
    return pl.pallas_call(
        kern,
        grid=(W_A // LANES, B),
        in_specs=[qspec, kspec, kspec, qspec, bspec],
        out_specs=qspec,
        out_shape=jax.ShapeDtypeStruct(q.shape, BF16),
        scratch_shapes=[pltpu.VMEM((2, 2 * QROWS, A_WIN), F32), pltpu.VMEM((2, 2 * QROWS, A_WIN), BF16)],
        compiler_params=_params(2),
        name="attn_a_window" if window else "attn_a",
    )(q, k, v, sg, bias)


def _sb_block(q2, kj, vj, tri, scr, slot, diag, first):
    z_scr, h_scr, a_scr, rs_scr = (r.at[slot] for r in scr[:4])
    dec_scr, acc_scr = scr[4:]
    rows = 2 * SB_BLK
    z_scr[...] = _dot_nt(q2, kj)
    if diag:
        ri = lax.broadcasted_iota(jnp.int32, (STRIP, SB_BLK), 0)
        ci = lax.broadcasted_iota(jnp.int32, (STRIP, SB_BLK), 1)
    for r in range(0, rows, STRIP):
        z = z_scr[r:r + STRIP, :]
        sp = jnp.maximum(z, 0.0) + jnp.log2(1.0 + jnp.exp2(-jnp.abs(z)))
        z_scr[r:r + STRIP, :] = z - sp
        if diag:
            sp = jnp.where(ci < ri + r % SB_BLK, sp, 0.0)
        h_scr[r:r + STRIP, :] = sp.astype(BF16)
        rs_scr[r:r + STRIP, :] = jnp.broadcast_to(jnp.sum(sp, axis=-1, keepdims=True), (STRIP, LANES))
    inner = jnp.dot(h_scr[...], tri, preferred_element_type=F32)
    for r in range(0, rows, STRIP):
        x = z_scr[r:r + STRIP, :] - inner[r:r + STRIP, :]
        if first:
            dec_scr[r:r + STRIP, :] = rs_scr[r:r + STRIP, :]
        else:
            dec = dec_scr[r:r + STRIP, :]
            x = x - jnp.concatenate([dec, dec], axis=1)
            dec_scr[r:r + STRIP, :] = dec + rs_scr[r:r + STRIP, :]
        a = jnp.exp2(x)
        if diag:
            a = jnp.where(ci < ri + r % SB_BLK, a, 0.0)
        a_scr[r:r + STRIP, :] = a.astype(BF16)
    pv = jnp.dot(a_scr[...], vj, preferred_element_type=F32)
    if first:
        acc_scr[...] = pv
    else:
        acc_scr[...] += pv


def _sb_query_block(qm, kd, vd, kp_ref, vp_ref, n_past, scr):
    low = _low_half((SB_BLK, LANES))
    q2 = jnp.concatenate([_head_of_pair(qm, low, 0), _head_of_pair(qm, low, 1)], axis=0)
    r = lax.broadcasted_iota(jnp.int32, (SB_BLK, SB_BLK), 0)
    c = lax.broadcasted_iota(jnp.int32, (SB_BLK, SB_BLK), 1)
    tri = jnp.where(r > c, 1.0, 0.0).astype(BF16)
    dec_scr, acc_scr = scr[4:]
    _sb_block(q2, kd, vd, tri, scr, 0, True, True)
    if not (isinstance(n_past, int) and n_past == 0):

        def past(j, slot):
            r0 = j * SB_BLK if isinstance(j, int) else pl.multiple_of(j * SB_BLK, SB_BLK)
            kj = kp_ref[pl.ds(r0, SB_BLK), :].astype(BF16)
            vj = vp_ref[pl.ds(r0, SB_BLK), :].astype(BF16)
            _sb_block(q2, kj, vj, tri, scr, slot, False, False)

        def live():
            return (jnp.min(dec_scr[...]) < SB_DEAD).astype(jnp.int32)

        def body(c):
            past(c[0], 0)
            return c[0] - 1, live()

        past(n_past - 1, 1)
        lax.while_loop(lambda c: (c[0] >= 0) & (c[1] > 0), body,
                       (jnp.asarray(n_past - 2, jnp.int32), live()))
    return jnp.where(low, acc_scr[:SB_BLK, :], acc_scr[SB_BLK:, :])


def _sb_scratch():
    rows = 2 * SB_BLK
    return [pltpu.VMEM((2, rows, SB_BLK), F32), pltpu.VMEM((2, rows, SB_BLK), BF16),
            pltpu.VMEM((2, rows, SB_BLK), BF16), pltpu.VMEM((2, rows, LANES), F32),
            pltpu.VMEM((rows, LANES), F32), pltpu.VMEM((rows, LANES), F32)]


def _attn_b_kernel(q_ref, k_ref, v_ref, sg_ref, o_ref, *scr):
    def qblock(i, c):
        rows = pl.ds(i * SB_BLK if isinstance(i, int) else pl.multiple_of(i * SB_BLK, SB_BLK), SB_BLK)
        o = _sb_query_block(q_ref[rows, :], k_ref[rows, :], v_ref[rows, :], k_ref, v_ref, i, scr)
        o_ref[rows, :] = (o * sg_ref[rows, :].astype(F32)).astype(BF16)
        return c

    qblock(0, 0)
    lax.fori_loop(1, q_ref.shape[0] // SB_BLK, qblock, 0)


def _attn_b_cached_kernel(q_ref, kd_ref, vd_ref, kp_ref, vp_ref, sg_ref, o_ref, *scr):
    o = _sb_query_block(q_ref[...], kd_ref[...], vd_ref[...], kp_ref, vp_ref,
                        kp_ref.shape[0] // SB_BLK, scr)
    o_ref[...] = (o * sg_ref[...].astype(F32)).astype(BF16)


def _attn_b(q, k, v, sg):
    B, S, _ = q.shape
    spec = pl.BlockSpec((None, S, LANES), lambda p, b: (b, 0, p))
    return pl.pallas_call(
        _attn_b_kernel,
        grid=(W_B // LANES, B),
        in_specs=[spec] * 4,
        out_specs=spec,
        out_shape=jax.ShapeDtypeStruct(q.shape, BF16),
        scratch_shapes=_sb_scratch(),
        compiler_params=_params(2),
        name="attn_b",
    )(q, k, v, sg)


def _attn_b_cached(q, kd, vd, kp, vp, sg):
    B = q.shape[0]
    P = kp.shape[1]
    spec = pl.BlockSpec((None, SB_BLK, LANES), lambda p, b: (b, 0, p))
    pspec = pl.BlockSpec((None, P, LANES), lambda p, b: (b, 0, p))
    return pl.pallas_call(
        _attn_b_cached_kernel,
        grid=(W_B // LANES, B),
        in_specs=[spec, spec, spec, pspec, pspec, spec],
        out_specs=spec,
        out_shape=jax.ShapeDtypeStruct(q.shape, BF16),
        scratch_shapes=_sb_scratch(),
        compiler_params=_params(2),
        name="attn_b_cached",
    )(q, kd, vd, kp, vp, sg)


def _rope_table_kernel(inv_ref, cos_ref, sin_ref, *, pos0, period):
    shape = cos_ref.shape
    row = lax.broadcasted_iota(jnp.int32, shape, 0)
    lane = lax.broadcasted_iota(jnp.int32, shape, 1)
    ang = (pos0 + row % period).astype(F32) * inv_ref[...]
    cos_ref[...] = jnp.cos(ang)
    s = jnp.sin(ang)
    sin_ref[...] = jnp.where(lane % HEAD_DIM < HEAD_DIM // 2, -s, s)


def _rope_tables(n_rows, pos0, period):
    half = HEAD_DIM // 2
    inv = ROPE_THETA ** (-jnp.arange(half, dtype=F32) * (2.0 / HEAD_DIM))
    inv = jnp.tile(inv, LANES // half)[None, :]
    shape = jax.ShapeDtypeStruct((n_rows, LANES), F32)
    return pl.pallas_call(
        functools.partial(_rope_table_kernel, pos0=pos0, period=period),
        out_shape=[shape, shape],
        name="rope_table",
    )(inv)


def _proj_c_kernel(x_ref, g_ref, w_ref, cos_ref, sin_ref, q_o, k_o, v_o, sg_o, kf_o, vf_o,
                   *, tiles_per_seq):
    x = x_ref[...]
    xn = (x * _rms_scale(x) * g_ref[...]).astype(BF16)
    tm = x.shape[0]
    keep = kf_o.shape[0]

    def keep_tail(o_ref, u):
        o_ref[...] = u[tm - keep:, :]

    cos = cos_ref[...]
    sin = sin_ref[...]
    first = lax.broadcasted_iota(jnp.int32, cos.shape, 1) % HEAD_DIM < HEAD_DIM // 2

    def rope(u):
        partner = jnp.where(first, pltpu.roll(u, LANES - HEAD_DIM // 2, 1), pltpu.roll(u, HEAD_DIM // 2, 1))
        return u * cos + partner * sin

    blk = 4 * LANES
    for c0 in range(0, W_C, blk):
        u = jnp.dot(xn, w_ref[:, c0:c0 + blk], preferred_element_type=F32)
        for j in range(0, blk, LANES):
            q_o[:, c0 + j:c0 + j + LANES] = (rope(u[:, j:j + LANES]) * (SCALE * LOG2E)).astype(BF16)
    u = jnp.dot(xn, w_ref[:, W_C:W_C + KVW_C], preferred_element_type=F32)
    r = jnp.concatenate([rope(u[:, j:j + LANES]) for j in range(0, KVW_C, LANES)], axis=1)
    k_o[...] = r.astype(BF16)
    keep_tail(kf_o, r)
    u = jnp.dot(xn, w_ref[:, W_C + KVW_C:W_C + 2 * KVW_C], preferred_element_type=F32)
    v_o[...] = u.astype(BF16)
    keep_tail(vf_o, u)
    g0 = W_C + 2 * KVW_C
    for c0 in range(0, W_C, blk):
        u = jnp.dot(xn, w_ref[:, g0 + c0:g0 + c0 + blk], preferred_element_type=F32)
        sg_o[:, c0:c0 + blk] = _silu(u).astype(BF16)


def _proj_c(x2d, g_pre, w_bf16, cos, sin, tm, tiles_per_seq, keep):
    T = x2d.shape[0]
    n_tiles = T // tm
    nt = cos.shape[0] // tm
    row = lambda w: pl.BlockSpec((tm, w), lambda i: (i, 0))
    const = lambda a: pl.BlockSpec(a.shape, lambda i: (0, 0), pipeline_mode=pl.Buffered(1))
    tab = pl.BlockSpec((tm, LANES), lambda i: (i % nt, 0))
    tail = pl.BlockSpec((keep, KVW_C), lambda i: (i // tiles_per_seq, 0))
    sd = jax.ShapeDtypeStruct
    tail_shape = sd((n_tiles // tiles_per_seq * keep, KVW_C), F32)
    return pl.pallas_call(
        functools.partial(_proj_c_kernel, tiles_per_seq=tiles_per_seq),
        grid=(n_tiles,),
        in_specs=[row(D_MODEL), const(g_pre), const(w_bf16), tab, tab],
        out_specs=[row(W_C), row(KVW_C), row(KVW_C), row(W_C), tail, tail],
        out_shape=[sd((T, W_C), BF16), sd((T, KVW_C), BF16), sd((T, KVW_C), BF16), sd((T, W_C), BF16),
                   tail_shape, tail_shape],
        compiler_params=_params(1, "arbitrary"),
        name="proj_c",
    )(x2d, g_pre, w_bf16, cos, sin)


def _attn_c_step(q_ref, k_ref, v_ref, sg_ref, sink_ref, o_ref, s_scr, p_scr, e_scr,
                 q0, k0, width, mcol0, n_valid):
    kw = k_ref[pl.ds(k0, width), :]
    vw = v_ref[pl.ds(k0, width), :]
    i = lax.broadcasted_iota(jnp.int32, (QROWS, width), 0) // CHUNK
    j = lax.broadcasted_iota(jnp.int32, (QROWS, width), 1)
    jc = (j + mcol0) // CHUNK
    valid = (jc <= i + WINDOW_C // CHUNK) & (jc >= i)
    if n_valid is not None:
        valid = valid & (j < n_valid)
    mask = jnp.where(valid, 0.0, NEG_INF)
    low = _low_half((QROWS, LANES))
    pair = pl.program_id(0)
    rows = pl.ds(q0, QROWS)
    blocks = [(hh, g) for hh in range(2) for g in range(G_C)]
    q2 = jnp.concatenate([_head_of_pair(q_ref[rows, g * LANES:(g + 1) * LANES], low, hh) for hh, g in blocks],
                         axis=0)
    z = _dot_nt(q2, kw)
    for bi, (hh, g) in enumerate(blocks):
        r0 = bi * QROWS
        s_scr[r0:r0 + QROWS, :width] = z[r0:r0 + QROWS] + mask
        sink = sink_ref[(2 * pair + hh) * G_C + g] * LOG2E
        for r in range(r0, r0 + QROWS, STRIP):
            x = s_scr[r:r + STRIP, :width]
            mx = jnp.maximum(jnp.max(x, axis=-1, keepdims=True), sink)
            p_scr[r:r + STRIP, :width] = jnp.exp2(x - mx).astype(BF16)
            e_scr[r:r + STRIP, :] = jnp.broadcast_to(jnp.exp2(sink - mx), (STRIP, LANES))
    o2 = jnp.dot(p_scr[:, :width], jnp.concatenate([vw, jnp.ones_like(vw)], axis=1),
                 preferred_element_type=F32)
    for g in range(G_C):
        t0, b0 = g * QROWS, (G_C + g) * QROWS
        top = o2[t0:t0 + QROWS, :LANES] / (o2[t0:t0 + QROWS, LANES:] + e_scr[t0:t0 + QROWS, :])
        bot = o2[b0:b0 + QROWS, :LANES] / (o2[b0:b0 + QROWS, LANES:] + e_scr[b0:b0 + QROWS, :])
        cols = slice(g * LANES, (g + 1) * LANES)
        o_ref[rows, cols] = (jnp.where(low, top, bot) * sg_ref[rows, cols].astype(F32)).astype(BF16)


def _attn_c_kernel(sink_ref, q_ref, k_ref, v_ref, sg_ref, o_ref, s_scr, p_scr, e_scr, *, n_steps):
    refs = (q_ref, k_ref, v_ref, sg_ref, sink_ref, o_ref)

    def step(m, slot):
        scr = (s_scr.at[slot], p_scr.at[slot], e_scr.at[slot])
        if isinstance(m, int) and m == 0:
            _attn_c_step(*refs, *scr, 0, 0, QROWS, C_WIN - QROWS, None)
        elif isinstance(m, int):
            _attn_c_step(*refs, *scr, m * QROWS, (m - 1) * QROWS, C_WIN, 0, None)
        else:
            _attn_c_step(*refs, *scr, pl.multiple_of(m * QROWS, QROWS),
                         pl.multiple_of((m - 1) * QROWS, QROWS), C_WIN, 0, None)

    step(0, 0)
    _paired_steps(1, n_steps, step)


def _attn_c_window_kernel(sink_ref, q_ref, k_ref, v_ref, sg_ref, o_ref, s_scr, p_scr, e_scr, *, n_valid):
    _attn_c_step(q_ref, k_ref, v_ref, sg_ref, sink_ref, o_ref, s_scr.at[0], p_scr.at[0], e_scr.at[0],
                 0, 0, C_WIN, 0, n_valid)


def _attn_c(q, k, v, sg, sinks, n_valid=None):
    B, Sq, _ = q.shape
    Sk = k.shape[1]
    qw = W_C // (KVW_C // LANES)
    qspec = pl.BlockSpec((None, Sq, qw), lambda p, b: (b, 0, p))
    kspec = pl.BlockSpec((None, Sk, LANES), lambda p, b: (b, 0, p))
    sspec = pl.BlockSpec(memory_space=pltpu.SMEM)
    if n_valid is None:
        kern = functools.partial(_attn_c_kernel, n_steps=Sq // QROWS)
    else:
        kern = functools.partial(_attn_c_window_kernel, n_valid=n_valid)
    stacked = 2 * G_C * QROWS
    return pl.pallas_call(
        kern,
        grid=(KVW_C // LANES, B),
        in_specs=[sspec, qspec, kspec, kspec, qspec],
        out_specs=qspec,
        out_shape=jax.ShapeDtypeStruct(q.shape, BF16),
        scratch_shapes=[pltpu.VMEM((2, stacked, C_WIN), F32), pltpu.VMEM((2, stacked, C_WIN), BF16),
                        pltpu.VMEM((2, stacked, LANES), F32)],
        compiler_params=_params(2),
        name="attn_c" if n_valid is None else "attn_c_window",
    )(sinks, q, k, v, sg)


def _c_head_order():
    heads = [(2 * P + hh) * G_C + g for P in range(KV_C // 2) for g in range(G_C) for hh in range(2)]
    return jnp.asarray([h * HEAD_DIM + d for h in heads for d in range(HEAD_DIM)], jnp.int32)


def _pad_rows(x, n):
    return jnp.pad(x, ((0, 0), (0, n - x.shape[1]), (0, 0)))


def _heads(x2d, B, h):
    return x2d.reshape(B, x2d.shape[0] // B, h, HEAD_DIM)


def _ab_layer(x, cache, g_pre, w_in, w_out, g_post, table):
    B, S, _ = x.shape
    T = B * S
    x2d = x.reshape(T, D_MODEL)
    r3 = lambda t: t.reshape(B, S, -1)
    cached = lambda t, h: t.reshape(B, -1, h, HEAD_DIM)
    if cache is None:
        tm = min(S, 512)
        la = min(LEFT_CHUNKS_A * CHUNK, S)
        assert S % tm == 0 and la <= tm
        proj = _proj_ab(x2d, g_pre[None, :], w_in.astype(BF16), tm, S // tm, la)
        qa, ka, va, sga, qb, kb, vb, sgb = proj[:8]
        oa = _attn_a(r3(qa), r3(ka), r3(va), r3(sga), _bias_table(table, A_WIN), window=False)
        ob = _attn_b(r3(qb), r3(kb), r3(vb), r3(sgb))
    else:
        ca_k, ca_v, cb_k, cb_v = cache
        la = ca_k.shape[1]
        assert la == LEFT_CHUNKS_A * CHUNK and S <= CHUNK and cb_k.shape[1] % SB_BLK == 0
        tm = T
        proj = _proj_ab(x2d, g_pre[None, :], w_in.astype(BF16), tm, 1, tm)
        qa, ka, va, sga, qb, kb, vb, sgb = proj[:8]
        flat = lambda c: c.reshape(B, c.shape[1], -1)
        win_k = _pad_rows(jnp.concatenate([flat(ca_k).astype(BF16), r3(ka)], axis=1), A_WIN)
        win_v = _pad_rows(jnp.concatenate([flat(ca_v).astype(BF16), r3(va)], axis=1), A_WIN)
        oa = _attn_a(_pad_rows(r3(qa), QROWS), win_k, win_v, _pad_rows(r3(sga), QROWS),
                     _bias_table(table, la + S), window=True)[:, :S]
        pb = lambda t: _pad_rows(r3(t), SB_BLK)
        ob = _attn_b_cached(pb(qb), pb(kb), pb(vb), flat(cb_k), flat(cb_v), pb(sgb))[:, :S]
    rows = (cached(proj[8], H_A), cached(proj[9], H_A), cached(proj[10], H_B), cached(proj[11], H_B))
    y = _merge([oa.reshape(T, W_A), ob.reshape(T, W_B)], w_out.astype(BF16), g_post[None, :], x2d, tm)
    return y.reshape(B, S, D_MODEL), rows


def _c_layer(x, cache, past, g_pre, w_in, sinks, w_out, g_post):
    B, S, _ = x.shape
    T = B * S
    x2d = x.reshape(T, D_MODEL)
    order = _c_head_order()
    g0 = W_C + 2 * KVW_C
    w_perm = jnp.concatenate([w_in[:, :W_C][:, order], w_in[:, W_C:g0], w_in[:, g0:][:, order]], axis=1)
    r3 = lambda t: t.reshape(B, S, -1)
    if cache is None:
        tm = min(S, 512)
        lc = min(WINDOW_C, S)
        assert S % tm == 0 and lc <= tm
        cos, sin = _rope_tables(S, 0, S)
        q, k, v, sg, kf, vf = _proj_c(x2d, g_pre[None, :], w_perm.astype(BF16), cos, sin, tm, S // tm, lc)
        o = _attn_c(r3(q), r3(k), r3(v), r3(sg), sinks)
    else:
        tm = T
        cos, sin = _rope_tables(T, past, S)
        q, k, v, sg, kf, vf = _proj_c(x2d, g_pre[None, :], w_perm.astype(BF16), cos, sin, tm, 1, tm)
        cc_k, cc_v = cache
        lc = cc_k.shape[1]
        assert lc == WINDOW_C and S <= CHUNK
        flat = lambda c: c.reshape(B, lc, -1).astype(BF16)
        win_k = _pad_rows(jnp.concatenate([flat(cc_k), r3(k)], axis=1), C_WIN)
        win_v = _pad_rows(jnp.concatenate([flat(cc_v), r3(v)], axis=1), C_WIN)
        o = _attn_c(_pad_rows(r3(q), QROWS), win_k, win_v, _pad_rows(r3(sg), QROWS), sinks,
                    n_valid=lc + S)[:, :S]
    rows = (_heads(kf, B, KV_C), _heads(vf, B, KV_C))
    y = _merge([o.reshape(T, W_C)], w_out[order, :].astype(BF16), g_post[None, :], x2d, tm)
    return y.reshape(B, S, D_MODEL), rows


def kernel(x_prompt, x_sample, cache_a_k, cache_a_v, cache_b_k, cache_b_v, cache_c_k, cache_c_v,
           ab_norm_pre, ab_w_in, ab_w_out, ab_norm_post, a_rel_bias,
           c_norm_pre, c_w_in, c_sinks, c_w_out, c_norm_post):
    past = cache_b_k.shape[2]
    depth = ab_w_in.shape[0] + c_w_in.shape[0]
    yp, ys = x_prompt, x_sample
    ab_p, ab_s, c_p, c_s = [], [], [], []
    for layer in range(depth):
        i = layer // 2
        if layer % 2 == 0:
            w = (ab_norm_pre[i], ab_w_in[i], ab_w_out[i], ab_norm_post[i], a_rel_bias[i])
            yp, rows = _ab_layer(yp, None, *w)
            ab_p.append(rows)
            ys, rows = _ab_layer(ys, (cache_a_k[i], cache_a_v[i], cache_b_k[i], cache_b_v[i]), *w)
            ab_s.append(rows)
        else:
            w = (c_norm_pre[i], c_w_in[i], c_sinks[i], c_w_out[i], c_norm_post[i])
            yp, rows = _c_layer(yp, None, past, *w)
            c_p.append(rows)
            ys, rows = _c_layer(ys, (cache_c_k[i], cache_c_v[i]), past, *w)
            c_s.append(rows)
    st = lambda rows, j: jnp.stack([r[j] for r in rows])
    return (yp, ys,
            st(ab_p, 0), st(ab_p, 1), st(ab_p, 2), st(ab_p, 3), st(c_p, 0), st(c_p, 1),
            st(ab_s, 0), st(ab_s, 1), st(ab_s, 2), st(ab_s, 3), st(c_s, 0), st(c_s, 1))
```

```python
import functools

import jax
import jax.numpy as jnp
from jax import lax
from jax.experimental import pallas as pl
from jax.experimental.pallas import tpu as pltpu

D_MODEL = 1024
CHUNK = 64
HEAD_DIM = 64
H_A = 8
H_B = 8
LEFT_CHUNKS_A = 8
REL_CLIP_A = 128
N_REL_A = 2 * REL_CLIP_A + 1
W_A = H_A * HEAD_DIM
W_B = H_B * HEAD_DIM
H_C = 16
KV_C = 4
G_C = H_C // KV_C
WINDOW_C = 128
W_C = H_C * HEAD_DIM
KVW_C = KV_C * HEAD_DIM
ROPE_THETA = 10000.0
RMS_EPS = 1e-6
NEG_INF = -1e30
SCALE = HEAD_DIM ** -0.5

LANES = 128
QROWS = 128
A_WIN = (LEFT_CHUNKS_A + 2) * CHUNK
A_TAB = 384
A_ROLLW = A_WIN + QROWS
C_WIN = WINDOW_C + QROWS
STRIP = 32
LOG2E = 1.4426950408889634
SB_BLK = 256
SB_DEAD = 104.0 * LOG2E
VMEM_LIMIT = 48 * 1024 * 1024

F32 = jnp.float32
BF16 = jnp.bfloat16


def _params(n_axes, semantics="parallel"):
    return pltpu.CompilerParams(dimension_semantics=(semantics,) * n_axes,
                                vmem_limit_bytes=VMEM_LIMIT)


def _rms_scale(x):
    return lax.rsqrt(jnp.mean(x * x, axis=-1, keepdims=True) + RMS_EPS)


def _silu(u):
    return u / (1.0 + jnp.exp(-u))


def _low_half(shape):
    return lax.broadcasted_iota(jnp.int32, shape, 1) < HEAD_DIM


def _head_of_pair(x_bf16, low, h):
    xf = x_bf16.astype(F32)
    return jnp.where(low if h == 0 else jnp.logical_not(low), xf, 0.0).astype(BF16)


def _dot_nt(a, b):
    return lax.dot_general(a, b, (((1,), (1,)), ((), ())), preferred_element_type=F32)


def _store_heads(o_ref, u, n_heads):
    for h in range(n_heads):
        o_ref[pl.ds(h, u.shape[0], stride=n_heads), :] = u[:, h * HEAD_DIM:(h + 1) * HEAD_DIM]


def _proj_ab_kernel(x_ref, g_ref, w_ref, qa, ka, va, sga, qb, kb, vb, sgb, kaf, vaf, kbf, vbf,
                    *, tiles_per_seq, keep_a):
    x = x_ref[...]
    xn = (x * _rms_scale(x) * g_ref[...]).astype(BF16)
    tm = x.shape[0]

    def col(j):
        return jnp.dot(xn, w_ref[:, j * W_A:(j + 1) * W_A], preferred_element_type=F32)

    qa[...] = (col(0) * (SCALE * LOG2E)).astype(BF16)
    u = col(1); ka[...] = u.astype(BF16); _store_heads(kaf, u[tm - keep_a:, :], H_A)
    u = col(2); va[...] = u.astype(BF16); _store_heads(vaf, u[tm - keep_a:, :], H_A)
    sga[...] = _silu(col(3)).astype(BF16)
    qb[...] = (col(4) * (SCALE * LOG2E)).astype(BF16)
    u = col(5); kb[...] = u.astype(BF16); _store_heads(kbf, u, H_B)
    u = col(6); vb[...] = u.astype(BF16); _store_heads(vbf, u, H_B)
    sgb[...] = _silu(col(7)).astype(BF16)


def _proj_ab(x2d, g_pre, w_bf16, tm, tiles_per_seq, keep_a):
    T = x2d.shape[0]
    n_tiles = T // tm
    row = lambda w: pl.BlockSpec((tm, w), lambda i: (i, 0))
    const = lambda a: pl.BlockSpec(a.shape, lambda i: (0, 0), pipeline_mode=pl.Buffered(1))
    bshape = jax.ShapeDtypeStruct((T, W_A), BF16)
    a_spec = pl.BlockSpec((keep_a * H_A, HEAD_DIM), lambda i: (i // tiles_per_seq, 0))
    a_shape = jax.ShapeDtypeStruct((n_tiles // tiles_per_seq * keep_a * H_A, HEAD_DIM), F32)
    b_spec = pl.BlockSpec((tm * H_B, HEAD_DIM), lambda i: (i, 0))
    b_shape = jax.ShapeDtypeStruct((T * H_B, HEAD_DIM), F32)
    return pl.pallas_call(
        functools.partial(_proj_ab_kernel, tiles_per_seq=tiles_per_seq, keep_a=keep_a),
        grid=(n_tiles,),
        in_specs=[row(D_MODEL), const(g_pre), const(w_bf16)],
        out_specs=[row(W_A)] * 8 + [a_spec, a_spec, b_spec, b_spec],
        out_shape=[bshape] * 8 + [a_shape, a_shape, b_shape, b_shape],
        compiler_params=_params(1, "arbitrary"),
        name="proj_ab",
    )(x2d, g_pre, w_bf16)


def _merge_body(h_refs, w_ref, g_ref, x_ref, y_ref):
    acc = None
    off = 0
    for h in h_refs:
        k = h.shape[1]
        d = jnp.dot(h[...], w_ref[off:off + k, :], preferred_element_type=F32)
        acc = d if acc is None else acc + d
        off += k
    y = x_ref[...] + acc * _rms_scale(acc) * g_ref[...]
    y_ref[...] = y
    return y


def _merge_kernel(*refs, n_parts):
    _merge_body(refs[:n_parts], *refs[n_parts:])


def _merge(parts, w_bf16, g_post, x2d, tm):
    T = x2d.shape[0]
    row = lambda w: pl.BlockSpec((tm, w), lambda i: (i, 0))
    full = lambda a: pl.BlockSpec(a.shape, lambda i: (0, 0))
    return pl.pallas_call(
        functools.partial(_merge_kernel, n_parts=len(parts)),
        grid=(T // tm,),
        in_specs=[row(p.shape[1]) for p in parts] + [full(w_bf16), full(g_post), row(D_MODEL)],
        out_specs=row(D_MODEL),
        out_shape=jax.ShapeDtypeStruct((T, D_MODEL), F32),
        compiler_params=_params(1),
        name="merge",
    )(*parts, w_bf16, g_post, x2d)


def _bias_kernel(tab_ref, out_ref, *, n_valid):
    m = lax.broadcasted_iota(jnp.int32, (A_TAB, A_ROLLW), 1)
    t = lax.broadcasted_iota(jnp.int32, (A_TAB, A_ROLLW), 0)
    idx = jnp.where(m <= A_WIN, jnp.clip(A_WIN - QROWS - m, -REL_CLIP_A, REL_CLIP_A) + REL_CLIP_A,
                    2 * REL_CLIP_A)
    onehot = jnp.where(t == idx, 1.0, 0.0).astype(BF16)
    tab = tab_ref[...]
    hi = tab.astype(BF16)
    r1 = tab - hi.astype(F32)
    mid = r1.astype(BF16)
    lo = (r1 - mid.astype(F32)).astype(BF16)
    src = (jnp.dot(hi, onehot, preferred_element_type=F32)
           + jnp.dot(mid, onehot, preferred_element_type=F32)
           + jnp.dot(lo, onehot, preferred_element_type=F32))
    i = lax.broadcasted_iota(jnp.int32, (QROWS, A_WIN), 0)
    j = lax.broadcasted_iota(jnp.int32, (QROWS, A_WIN), 1)
    qc = LEFT_CHUNKS_A + i // CHUNK
    kc = j // CHUNK
    valid = (kc <= qc) & (kc >= qc - LEFT_CHUNKS_A) & (j < n_valid)
    for h in range(H_A):
        row = jnp.broadcast_to(src[h:h + 1, :], (QROWS, A_ROLLW))
        toep = pltpu.roll(row, 0, 1, stride=1, stride_axis=0)
        out_ref[h * QROWS:(h + 1) * QROWS, :] = jnp.where(valid, toep[:, :A_WIN] * LOG2E, NEG_INF)


def _bias_table(table, n_valid):
    tab = jnp.pad(table, ((0, 0), (0, A_TAB - N_REL_A)))
    return pl.pallas_call(
        functools.partial(_bias_kernel, n_valid=n_valid),
        out_shape=jax.ShapeDtypeStruct((H_A * QROWS, A_WIN), F32),
        name="a_bias",
    )(tab)


def _pipeline(n_units, n_static, period, stages):
    depth = len(stages)

    def tick(t, tmod):
        for lag in reversed(range(depth)):
            u = t - lag
            if not isinstance(u, int) or 0 <= u < n_units:
                stages[lag](u, (tmod - lag) % period)

    first = n_static + depth - 1
    for t in range(min(first, n_units + depth - 1)):
        tick(t, t % period)
    groups = max(n_units - first, 0) // period

    def body(g, c):
        t = first + period * g
        for d in range(period):
            tick(t + d, (first + d) % period)
        return c

    if groups:
        lax.fori_loop(0, groups, body, 0)
    for t in range(max(first, first + period * groups), n_units + depth - 1):
        tick(t, t % period)


def _softmax_strips(s_scr, p_scr, width):
    for r in range(0, s_scr.shape[0], STRIP):
        x = s_scr[r:r + STRIP, :width]
        p_scr[r:r + STRIP, :width] = jnp.exp2(x - jnp.max(x, axis=-1, keepdims=True)).astype(BF16)


def _attn_a_stages(q_ref, k_ref, v_ref, sg_ref, bias_ref, o_ref, s_scr, p_scr, geom):
    low = _low_half((QROWS, LANES))

    def qk(m, slot):
        q0, k0, width, bcol0 = geom(m)
        qm = q_ref[pl.ds(q0, QROWS), :]
        q2 = jnp.concatenate([_head_of_pair(qm, low, 0), _head_of_pair(qm, low, 1)], axis=0)
        s_scr[slot, :, :width] = _dot_nt(q2, k_ref[pl.ds(k0, width), :]) + bias_ref[:, bcol0:bcol0 + width]

    def sm(m, slot):
        _softmax_strips(s_scr.at[slot], p_scr.at[slot], geom(m)[2])

    def pv(m, slot):
        q0, k0, width, _ = geom(m)
        vw = v_ref[pl.ds(k0, width), :]
        o2 = jnp.dot(p_scr[slot, :, :width], jnp.concatenate([vw, jnp.ones_like(vw)], axis=1),
                     preferred_element_type=F32)
        top, bot = o2[:QROWS], o2[QROWS:]
        o = jnp.where(low, top[:, :LANES] / top[:, LANES:], bot[:, :LANES] / bot[:, LANES:])
        o_ref[pl.ds(q0, QROWS), :] = (o * sg_ref[pl.ds(q0, QROWS), :].astype(F32)).astype(BF16)

    return qk, sm, pv


def _attn_a_kernel(q_ref, k_ref, v_ref, sg_ref, bias_ref, o_ref, s_scr, p_scr, *, n_steps):
    lead = A_WIN // QROWS - 1

    def geom(m):
        if isinstance(m, int) and m < lead:
            return m * QROWS, 0, (m + 1) * QROWS, (lead - m) * QROWS
        if isinstance(m, int):
            return m * QROWS, (m - lead) * QROWS, A_WIN, 0
        return pl.multiple_of(m * QROWS, QROWS), pl.multiple_of((m - lead) * QROWS, QROWS), A_WIN, 0

    _pipeline(n_steps, lead, 2, _attn_a_stages(q_ref, k_ref, v_ref, sg_ref, bias_ref, o_ref, s_scr, p_scr, geom))


def _attn_a_window_kernel(q_ref, k_ref, v_ref, sg_ref, bias_ref, o_ref, s_scr, p_scr):
    _pipeline(1, 1, 2, _attn_a_stages(q_ref, k_ref, v_ref, sg_ref, bias_ref, o_ref, s_scr, p_scr,
                                      lambda m: (0, 0, A_WIN, 0)))


def _attn_a(q, k, v, sg, bias, window):
    B, Sq, _ = q.shape
    Sk = k.shape[1]
    qspec = pl.BlockSpec((None, Sq, LANES), lambda p, b: (b, 0, p))
    kspec = pl.BlockSpec((None, Sk, LANES), lambda p, b: (b, 0, p))
    bspec = pl.BlockSpec((2 * QROWS, A_WIN), lambda p, b: (p, 0))
    kern = _attn_a_window_kernel if window else functools.partial(_attn_a_kernel, n_steps=Sq // QROWS)
    return pl.pallas_call(
        kern,
        grid=(W_A // LANES, B),
        in_specs=[qspec, kspec, kspec, qspec, bspec],
        out_specs=qspec,
        out_shape=jax.ShapeDtypeStruct(q.shape, BF16),
        scratch_shapes=[pltpu.VMEM((2, 2 * QROWS, A_WIN), F32), pltpu.VMEM((2, 2 * QROWS, A_WIN), BF16)],
        compiler_params=_params(2),
        name="attn_a_window" if window else "attn_a",
    )(q, k, v, sg, bias)


def _sb_stages(scr, unit):
    z_scr, h_scr, a_scr, rs_scr = scr
    rows = 2 * SB_BLK
    low = _low_half((SB_BLK, LANES))
    ri = lax.broadcasted_iota(jnp.int32, (STRIP, SB_BLK), 0)
    ci = lax.broadcasted_iota(jnp.int32, (STRIP, SB_BLK), 1)
    r = lax.broadcasted_iota(jnp.int32, (SB_BLK, SB_BLK), 0)
    c = lax.broadcasted_iota(jnp.int32, (SB_BLK, SB_BLK), 1)
    tri = jnp.where(r > c, 1.0, 0.0).astype(BF16)

    def scores(u, us):
        q_ref, q0, k_ref, _, k0, _, _, _ = unit(u, us)
        qm = q_ref[pl.ds(q0, SB_BLK), :]
        q2 = jnp.concatenate([_head_of_pair(qm, low, 0), _head_of_pair(qm, low, 1)], axis=0)
        z_scr[us % 4] = _dot_nt(q2, k_ref[pl.ds(k0, SB_BLK), :].astype(BF16))

    def terms(u, us):
        diag = unit(u, us)[5]
        zs, hs, rs = z_scr.at[us % 4], h_scr.at[us % 2], rs_scr.at[us % 2]
        for r0 in range(0, rows, STRIP):
            z = zs[r0:r0 + STRIP, :]
            sp = jnp.maximum(z, 0.0) + jnp.log2(1.0 + jnp.exp2(-jnp.abs(z)))
            zs[r0:r0 + STRIP, :] = z - sp
            if diag:
                sp = jnp.where(ci < ri + r0 % SB_BLK, sp, 0.0)
            hs[r0:r0 + STRIP, :] = sp.astype(BF16)
            rs[r0:r0 + STRIP, :] = jnp.broadcast_to(jnp.sum(sp, axis=-1, keepdims=True), (STRIP, LANES))

    def weights(u, us):
        diag, dec = unit(u, us)[5:7]
        zs, rs, a_s = z_scr.at[us % 4], rs_scr.at[us % 2], a_scr.at[us % 2]
        inner = jnp.dot(h_scr[us % 2], tri, preferred_element_type=F32)
        for r0 in range(0, rows, STRIP):
            x = zs[r0:r0 + STRIP, :] - inner[r0:r0 + STRIP, :]
            if diag:
                dec[r0:r0 + STRIP, :] = rs[r0:r0 + STRIP, :]
            else:
                d = dec[r0:r0 + STRIP, :]
                x = x - jnp.concatenate([d, d], axis=1)
                dec[r0:r0 + STRIP, :] = d + rs[r0:r0 + STRIP, :]
            a = jnp.exp2(x)
            if diag:
                a = jnp.where(ci < ri + r0 % SB_BLK, a, 0.0)
            a_s[r0:r0 + STRIP, :] = a.astype(BF16)

    def values(u, us):
        _, _, _, v_ref, k0, diag, _, acc = unit(u, us)
        pv = jnp.dot(a_scr[us % 2], v_ref[pl.ds(k0, SB_BLK), :].astype(BF16), preferred_element_type=F32)
        if diag:
            acc[...] = pv
        else:
            acc[...] += pv

    return scores, terms, weights, values


def _sb_deeper(stages, unit_at, j0, dec):
    def live():
        return (jnp.min(dec[...]) < SB_DEAD).astype(jnp.int32)

    def body(c):
        for stage in stages:
            stage(c[0], 0)
        return c[0] - 1, live()

    lax.while_loop(lambda c: (c[0] >= 0) & (c[1] > 0), body, (jnp.asarray(j0, jnp.int32), live()))


def _sb_scratch(n_qblocks):
    rows = 2 * SB_BLK
    return [pltpu.VMEM((4, rows, SB_BLK), F32), pltpu.VMEM((2, rows, SB_BLK), BF16),
            pltpu.VMEM((2, rows, SB_BLK), BF16), pltpu.VMEM((2, rows, LANES), F32),
            pltpu.VMEM((n_qblocks, rows, LANES), F32), pltpu.VMEM((n_qblocks, rows, LANES), F32)]


def _sb_output(acc, sg):
    low = _low_half((SB_BLK, LANES))
    return (jnp.where(low, acc[:SB_BLK, :], acc[SB_BLK:, :]) * sg.astype(F32)).astype(BF16)


def _attn_b_kernel(q_ref, k_ref, v_ref, sg_ref, o_ref, z_scr, h_scr, a_scr, rs_scr, dec_all, acc_all):
    n_q = q_ref.shape[0] // SB_BLK
    scr = (z_scr, h_scr, a_scr, rs_scr)
    blk = lambda i: i * SB_BLK if isinstance(i, int) else pl.multiple_of(i * SB_BLK, SB_BLK)

    def unit(u, us):
        i = (u + 1) // 2
        diag = u % 2 == 1 or u == 0 if isinstance(u, int) else us % 2 == 1
        return q_ref, blk(i), k_ref, v_ref, blk(i if diag else i - 1), diag, dec_all.at[i], acc_all.at[i]

    _pipeline(2 * n_q - 1, 1, 4, _sb_stages(scr, unit))

    def finish(i, c):
        def unit_at(j, us):
            return q_ref, blk(i), k_ref, v_ref, blk(j), False, dec_all.at[i], acc_all.at[i]

        _sb_deeper(_sb_stages(scr, unit_at), unit_at, i - 2, dec_all.at[i])
        rows = pl.ds(blk(i), SB_BLK)
        o_ref[rows, :] = _sb_output(acc_all[i], sg_ref[rows, :])
        return c

    lax.fori_loop(0, n_q, finish, 0)


def _attn_b_cached_kernel(q_ref, kd_ref, vd_ref, kp_ref, vp_ref, sg_ref, o_ref,
                          z_scr, h_scr, a_scr, rs_scr, dec_all, acc_all):
    n_past = kp_ref.shape[0] // SB_BLK
    scr = (z_scr, h_scr, a_scr, rs_scr)

    def unit(u, us):
        if u == 0:
            return q_ref, 0, kd_ref, vd_ref, 0, True, dec_all.at[0], acc_all.at[0]
        return q_ref, 0, kp_ref, vp_ref, (n_past - 1) * SB_BLK, False, dec_all.at[0], acc_all.at[0]

    def unit_at(j, us):
        return (q_ref, 0, kp_ref, vp_ref, pl.multiple_of(j * SB_BLK, SB_BLK), False,
                dec_all.at[0], acc_all.at[0])

    _pipeline(2, 2, 4, _sb_stages(scr, unit))
    _sb_deeper(_sb_stages(scr, unit_at), unit_at, n_past - 2, dec_all.at[0])
    o_ref[...] = _sb_output(acc_all[0], sg_ref[...])


def _attn_b(q, k, v, sg):
    B, S, _ = q.shape
    spec = pl.BlockSpec((None, S, LANES), lambda p, b: (b, 0, p))
    return pl.pallas_call(
        _attn_b_kernel,
        grid=(W_B // LANES, B),
        in_specs=[spec] * 4,
        out_specs=spec,
        out_shape=jax.ShapeDtypeStruct(q.shape, BF16),
        scratch_shapes=_sb_scratch(S // SB_BLK),
        compiler_params=_params(2),
        name="attn_b",
    )(q, k, v, sg)


def _attn_b_cached(q, kd, vd, kp, vp, sg):
    B = q.shape[0]
    P = kp.shape[1]
    spec = pl.BlockSpec((None, SB_BLK, LANES), lambda p, b: (b, 0, p))
    pspec = pl.BlockSpec((None, P, LANES), lambda p, b: (b, 0, p))
    return pl.pallas_call(
        _attn_b_cached_kernel,
        grid=(W_B // LANES, B),
        in_specs=[spec, spec, spec, pspec, pspec, spec],
        out_specs=spec,
        out_shape=jax.ShapeDtypeStruct(q.shape, BF16),
        scratch_shapes=_sb_scratch(1),
        compiler_params=_params(2),
        name="attn_b_cached",
    )(q, kd, vd, kp, vp, sg)


def _rope_table_kernel(inv_ref, cos_ref, sin_ref, *, pos0, period):
    shape = cos_ref.shape
    row = lax.broadcasted_iota(jnp.int32, shape, 0)
    lane = lax.broadcasted_iota(jnp.int32, shape, 1)
    ang = (pos0 + row % period).astype(F32) * inv_ref[...]
    cos_ref[...] = jnp.cos(ang)
    s = jnp.sin(ang)
    sin_ref[...] = jnp.where(lane % HEAD_DIM < HEAD_DIM // 2, -s, s)


def _rope_tables(n_rows, pos0, period):
    half = HEAD_DIM // 2
    inv = ROPE_THETA ** (-jnp.arange(half, dtype=F32) * (2.0 / HEAD_DIM))
    inv = jnp.tile(inv, LANES // half)[None, :]
    shape = jax.ShapeDtypeStruct((n_rows, LANES), F32)
    return pl.pallas_call(
        functools.partial(_rope_table_kernel, pos0=pos0, period=period),
        out_shape=[shape, shape],
        name="rope_table",
    )(inv)


def _proj_c_kernel(*refs, n_parts):
    if n_parts:
        w_out, g_post, x_ref, g_ref, w_ref, cos_ref, sin_ref, y_ref = refs[n_parts:n_parts + 8]
        x = _merge_body(refs[:n_parts], w_out, g_post, x_ref, y_ref)
        _proj_c_body(x, g_ref, w_ref, cos_ref, sin_ref, *refs[n_parts + 8:])
    else:
        _proj_c_body(refs[0][...], *refs[1:])


def _proj_c_body(x, g_ref, w_ref, cos_ref, sin_ref, q_o, k_o, v_o, sg_o, kf_o, vf_o):
    xn = (x * _rms_scale(x) * g_ref[...]).astype(BF16)
    tm = x.shape[0]
    keep = kf_o.shape[0]

    def keep_tail(o_ref, u):
        o_ref[...] = u[tm - keep:, :]

    cos = cos_ref[...]
    sin = sin_ref[...]
    first = lax.broadcasted_iota(jnp.int32, cos.shape, 1) % HEAD_DIM < HEAD_DIM // 2

    def rope(u):
        partner = jnp.where(first, pltpu.roll(u, LANES - HEAD_DIM // 2, 1), pltpu.roll(u, HEAD_DIM // 2, 1))
        return u * cos + partner * sin

    blk = 4 * LANES
    for c0 in range(0, W_C, blk):
        u = jnp.dot(xn, w_ref[:, c0:c0 + blk], preferred_element_type=F32)
        for j in range(0, blk, LANES):
            q_o[:, c0 + j:c0 + j + LANES] = (rope(u[:, j:j + LANES]) * (SCALE * LOG2E)).astype(BF16)
    u = jnp.dot(xn, w_ref[:, W_C:W_C + KVW_C], preferred_element_type=F32)
    r = jnp.concatenate([rope(u[:, j:j + LANES]) for j in range(0, KVW_C, LANES)], axis=1)
    k_o[...] = r.astype(BF16)
    keep_tail(kf_o, r)
    u = jnp.dot(xn, w_ref[:, W_C + KVW_C:W_C + 2 * KVW_C], preferred_element_type=F32)
    v_o[...] = u.astype(BF16)
    keep_tail(vf_o, u)
    g0 = W_C + 2 * KVW_C
    for c0 in range(0, W_C, blk):
        u = jnp.dot(xn, w_ref[:, g0 + c0:g0 + c0 + blk], preferred_element_type=F32)
        sg_o[:, c0:c0 + blk] = _silu(u).astype(BF16)


def _proj_c(x2d, g_pre, w_bf16, cos, sin, tm, tiles_per_seq, keep, merge=None):
    parts, w_out, g_post, x_res = merge if merge is not None else ((), None, None, None)
    T = (x_res if merge is not None else x2d).shape[0]
    n_tiles = T // tm
    nt = cos.shape[0] // tm
    row = lambda w: pl.BlockSpec((tm, w), lambda i: (i, 0))
    const = lambda a: pl.BlockSpec(a.shape, lambda i: (0, 0), pipeline_mode=pl.Buffered(1))
    tab = pl.BlockSpec((tm, LANES), lambda i: (i % nt, 0))
    tail = pl.BlockSpec((keep, KVW_C), lambda i: (i // tiles_per_seq, 0))
    sd = jax.ShapeDtypeStruct
    tail_shape = sd((n_tiles // tiles_per_seq * keep, KVW_C), F32)
    if merge is not None:
        operands = [*parts, w_out, g_post, x_res]
        in_specs = [row(p.shape[1]) for p in parts] + [const(w_out), const(g_post), row(D_MODEL)]
        out_specs, out_shape = [row(D_MODEL)], [sd((T, D_MODEL), F32)]
    else:
        operands, in_specs, out_specs, out_shape = [x2d], [row(D_MODEL)], [], []
    return pl.pallas_call(
        functools.partial(_proj_c_kernel, n_parts=len(parts)),
        grid=(n_tiles,),
        in_specs=in_specs + [const(g_pre), const(w_bf16), tab, tab],
        out_specs=out_specs + [row(W_C), row(KVW_C), row(KVW_C), row(W_C), tail, tail],
        out_shape=out_shape + [sd((T, W_C), BF16), sd((T, KVW_C), BF16), sd((T, KVW_C), BF16),
                               sd((T, W_C), BF16), tail_shape, tail_shape],
        compiler_params=_params(1, "arbitrary"),
        name="merge_proj_c" if merge is not None else "proj_c",
    )(*operands, g_pre, w_bf16, cos, sin)


def _attn_c_stages(sink_ref, q_ref, k_ref, v_ref, sg_ref, o_ref, s_scr, p_scr, e_scr, geom, n_valid):
    low = _low_half((QROWS, LANES))
    blocks = [(hh, g) for hh in range(2) for g in range(G_C)]

    def qk(m, slot):
        q0, k0, width, mcol0 = geom(m)
        i = lax.broadcasted_iota(jnp.int32, (QROWS, width), 0) // CHUNK
        j = lax.broadcasted_iota(jnp.int32, (QROWS, width), 1)
        jc = (j + mcol0) // CHUNK
        valid = (jc <= i + WINDOW_C // CHUNK) & (jc >= i)
        if n_valid is not None:
            valid = valid & (j < n_valid)
        mask = jnp.where(valid, 0.0, NEG_INF)
        rows = pl.ds(q0, QROWS)
        q2 = jnp.concatenate([_head_of_pair(q_ref[rows, g * LANES:(g + 1) * LANES], low, hh)
                              for hh, g in blocks], axis=0)
        z = _dot_nt(q2, k_ref[pl.ds(k0, width), :])
        for bi in range(len(blocks)):
            s_scr[slot, bi * QROWS:(bi + 1) * QROWS, :width] = z[bi * QROWS:(bi + 1) * QROWS] + mask

    def sm(m, slot):
        width = geom(m)[2]
        pair = pl.program_id(0)
        for bi, (hh, g) in enumerate(blocks):
            sink = sink_ref[(2 * pair + hh) * G_C + g] * LOG2E
            for r in range(bi * QROWS, (bi + 1) * QROWS, STRIP):
                x = s_scr[slot, r:r + STRIP, :width]
                mx = jnp.maximum(jnp.max(x, axis=-1, keepdims=True), sink)
                p_scr[slot, r:r + STRIP, :width] = jnp.exp2(x - mx).astype(BF16)
                e_scr[slot, r:r + STRIP, :] = jnp.broadcast_to(jnp.exp2(sink - mx), (STRIP, LANES))

    def pv(m, slot):
        q0, k0, width, _ = geom(m)
        rows = pl.ds(q0, QROWS)
        vw = v_ref[pl.ds(k0, width), :]
        o2 = jnp.dot(p_scr[slot, :, :width], jnp.concatenate([vw, jnp.ones_like(vw)], axis=1),
                     preferred_element_type=F32)
        for g in range(G_C):
            t0, b0 = g * QROWS, (G_C + g) * QROWS
            top = o2[t0:t0 + QROWS, :LANES] / (o2[t0:t0 + QROWS, LANES:] + e_scr[slot, t0:t0 + QROWS, :])
            bot = o2[b0:b0 + QROWS, :LANES] / (o2[b0:b0 + QROWS, LANES:] + e_scr[slot, b0:b0 + QROWS, :])
            cols = slice(g * LANES, (g + 1) * LANES)
            o_ref[rows, cols] = (jnp.where(low, top, bot) * sg_ref[rows, cols].astype(F32)).astype(BF16)

    return qk, sm, pv


def _attn_c_kernel(sink_ref, q_ref, k_ref, v_ref, sg_ref, o_ref, s_scr, p_scr, e_scr, *, n_steps):
    def geom(m):
        if isinstance(m, int) and m == 0:
            return 0, 0, QROWS, C_WIN - QROWS
        if isinstance(m, int):
            return m * QROWS, (m - 1) * QROWS, C_WIN, 0
        return pl.multiple_of(m * QROWS, QROWS), pl.multiple_of((m - 1) * QROWS, QROWS), C_WIN, 0

    _pipeline(n_steps, 1, 2, _attn_c_stages(sink_ref, q_ref, k_ref, v_ref, sg_ref, o_ref,
                                            s_scr, p_scr, e_scr, geom, None))


def _attn_c_window_kernel(sink_ref, q_ref, k_ref, v_ref, sg_ref, o_ref, s_scr, p_scr, e_scr, *, n_valid):
    _pipeline(1, 1, 2, _attn_c_stages(sink_ref, q_ref, k_ref, v_ref, sg_ref, o_ref, s_scr, p_scr, e_scr,
                                      lambda m: (0, 0, C_WIN, 0), n_valid))


def _attn_c(q, k, v, sg, sinks, n_valid=None):
    B, Sq, _ = q.shape
    Sk = k.shape[1]
    qw = W_C // (KVW_C // LANES)
    qspec = pl.BlockSpec((None, Sq, qw), lambda p, b: (b, 0, p))
    kspec = pl.BlockSpec((None, Sk, LANES), lambda p, b: (b, 0, p))
    sspec = pl.BlockSpec(memory_space=pltpu.SMEM)
    if n_valid is None:
        kern = functools.partial(_attn_c_kernel, n_steps=Sq // QROWS)
    else:
        kern = functools.partial(_attn_c_window_kernel, n_valid=n_valid)
    stacked = 2 * G_C * QROWS
    return pl.pallas_call(
        kern,
        grid=(KVW_C // LANES, B),
        in_specs=[sspec, qspec, kspec, kspec, qspec],
        out_specs=qspec,
        out_shape=jax.ShapeDtypeStruct(q.shape, BF16),
        scratch_shapes=[pltpu.VMEM((2, stacked, C_WIN), F32), pltpu.VMEM((2, stacked, C_WIN), BF16),
                        pltpu.VMEM((2, stacked, LANES), F32)],
        compiler_params=_params(2),
        name="attn_c" if n_valid is None else "attn_c_window",
    )(sinks, q, k, v, sg)


def _c_head_order():
    heads = [(2 * P + hh) * G_C + g for P in range(KV_C // 2) for g in range(G_C) for hh in range(2)]
    return jnp.asarray([h * HEAD_DIM + d for h in heads for d in range(HEAD_DIM)], jnp.int32)


def _pad_rows(x, n):
    return jnp.pad(x, ((0, 0), (0, n - x.shape[1]), (0, 0)))


def _heads(x2d, B, h):
    return x2d.reshape(B, x2d.shape[0] // B, h, HEAD_DIM)


def _flush(act):
    if not isinstance(act, tuple):
        return act
    merge, B, S, tm = act
    return _merge(*merge, tm).reshape(B, S, D_MODEL)


def _ab_layer(x, cache, g_pre, w_in, w_out, g_post, table):
    x = _flush(x)
    B, S, _ = x.shape
    T = B * S
    x2d = x.reshape(T, D_MODEL)
    r3 = lambda t: t.reshape(B, S, -1)
    cached = lambda t, h: t.reshape(B, -1, h, HEAD_DIM)
    if cache is None:
        tm = min(S, 512)
        la = min(LEFT_CHUNKS_A * CHUNK, S)
        assert S % tm == 0 and la <= tm
        proj = _proj_ab(x2d, g_pre[None, :], w_in.astype(BF16), tm, S // tm, la)
        qa, ka, va, sga, qb, kb, vb, sgb = proj[:8]
        oa = _attn_a(r3(qa), r3(ka), r3(va), r3(sga), _bias_table(table, A_WIN), window=False)
        ob = _attn_b(r3(qb), r3(kb), r3(vb), r3(sgb))
    else:
        ca_k, ca_v, cb_k, cb_v = cache
        la = ca_k.shape[1]
        assert la == LEFT_CHUNKS_A * CHUNK and S <= CHUNK and cb_k.shape[1] % SB_BLK == 0
        tm = T
        proj = _proj_ab(x2d, g_pre[None, :], w_in.astype(BF16), tm, 1, tm)
        qa, ka, va, sga, qb, kb, vb, sgb = proj[:8]
        flat = lambda c: c.reshape(B, c.shape[1], -1)
        win_k = _pad_rows(jnp.concatenate([flat(ca_k).astype(BF16), r3(ka)], axis=1), A_WIN)
        win_v = _pad_rows(jnp.concatenate([flat(ca_v).astype(BF16), r3(va)], axis=1), A_WIN)
        oa = _attn_a(_pad_rows(r3(qa), QROWS), win_k, win_v, _pad_rows(r3(sga), QROWS),
                     _bias_table(table, la + S), window=True)[:, :S]
        pb = lambda t: _pad_rows(r3(t), SB_BLK)
        ob = _attn_b_cached(pb(qb), pb(kb), pb(vb), flat(cb_k), flat(cb_v), pb(sgb))[:, :S]
    rows = (cached(proj[8], H_A), cached(proj[9], H_A), cached(proj[10], H_B), cached(proj[11], H_B))
    merge = ([oa.reshape(T, W_A), ob.reshape(T, W_B)], w_out.astype(BF16), g_post[None, :], x2d)
    return (merge, B, S, tm), rows


def _c_layer(x, cache, past, g_pre, w_in, sinks, w_out, g_post):
    if isinstance(x, tuple):
        merge, B, S, _ = x
        x2d = None
    else:
        merge, (B, S, _) = None, x.shape
        x2d = x.reshape(B * S, D_MODEL)
    T = B * S
    order = _c_head_order()
    g0 = W_C + 2 * KVW_C
    w_perm = jnp.concatenate([w_in[:, :W_C][:, order], w_in[:, W_C:g0], w_in[:, g0:][:, order]], axis=1)
    r3 = lambda t: t.reshape(B, S, -1)
    if cache is None:
        tm = min(S, 512)
        lc = min(WINDOW_C, S)
        assert S % tm == 0 and lc <= tm
        cos, sin = _rope_tables(S, 0, S)
        proj = _proj_c(x2d, g_pre[None, :], w_perm.astype(BF16), cos, sin, tm, S // tm, lc, merge)
        x2d, (q, k, v, sg, kf, vf) = (proj[0], proj[1:]) if merge is not None else (x2d, proj)
        o = _attn_c(r3(q), r3(k), r3(v), r3(sg), sinks)
    else:
        tm = T
        cos, sin = _rope_tables(T, past, S)
        proj = _proj_c(x2d, g_pre[None, :], w_perm.astype(BF16), cos, sin, tm, 1, tm, merge)
        x2d, (q, k, v, sg, kf, vf) = (proj[0], proj[1:]) if merge is not None else (x2d, proj)
        cc_k, cc_v = cache
        lc = cc_k.shape[1]
        assert lc == WINDOW_C and S <= CHUNK
        flat = lambda c: c.reshape(B, lc, -1).astype(BF16)
        win_k = _pad_rows(jnp.concatenate([flat(cc_k), r3(k)], axis=1), C_WIN)
        win_v = _pad_rows(jnp.concatenate([flat(cc_v), r3(v)], axis=1), C_WIN)
        o = _attn_c(_pad_rows(r3(q), QROWS), win_k, win_v, _pad_rows(r3(sg), QROWS), sinks,
                    n_valid=lc + S)[:, :S]
    rows = (_heads(kf, B, KV_C), _heads(vf, B, KV_C))
    merge = ([o.reshape(T, W_C)], w_out[order, :].astype(BF16), g_post[None, :], x2d)
    return (merge, B, S, tm), rows


def kernel(x_prompt, x_sample, cache_a_k, cache_a_v, cache_b_k, cache_b_v, cache_c_k, cache_c_v,
           ab_norm_pre, ab_w_in, ab_w_out, ab_norm_post, a_rel_bias,
           c_norm_pre, c_w_in, c_sinks, c_w_out, c_norm_post):
    past = cache_b_k.shape[2]
    depth = ab_w_in.shape[0] + c_w_in.shape[0]
    yp, ys = x_prompt, x_sample
    ab_p, ab_s, c_p, c_s = [], [], [], []
    for layer in range(depth):
        i = layer // 2
        if layer % 2 == 0:
            w = (ab_norm_pre[i], ab_w_in[i], ab_w_out[i], ab_norm_post[i], a_rel_bias[i])
            yp, rows = _ab_layer(yp, None, *w)
            ab_p.append(rows)
            ys, rows = _ab_layer(ys, (cache_a_k[i], cache_a_v[i], cache_b_k[i], cache_b_v[i]), *w)
            ab_s.append(rows)
        else:
            w = (c_norm_pre[i], c_w_in[i], c_sinks[i], c_w_out[i], c_norm_post[i])
            yp, rows = _c_layer(yp, None, past, *w)
            c_p.append(rows)
            ys, rows = _c_layer(ys, (cache_c_k[i], cache_c_v[i]), past, *w)
            c_s.append(rows)
    st = lambda rows, j: jnp.stack([r[j] for r in rows])
    return (_flush(yp), _flush(ys),
            st(ab_p, 0), st(ab_p, 1), st(ab_p, 2), st(ab_p, 3), st(c_p, 0), st(c_p, 1),
            st(ab_s, 0), st(ab_s, 1), st(ab_s, 2), st(ab_s, 3), st(c_s, 0), st(c_s, 1))
```

```python
import functools

import jax
import jax.numpy as jnp
from jax import lax
from jax.experimental import pallas as pl
from jax.experimental.pallas import tpu as pltpu

D_MODEL = 1024
CHUNK = 64
HEAD_DIM = 64
H_A = 8
H_B = 8
LEFT_CHUNKS_A = 8
REL_CLIP_A = 128
N_REL_A = 2 * REL_CLIP_A + 1
W_A = H_A * HEAD_DIM
W_B = H_B * HEAD_DIM
H_C = 16
KV_C = 4
G_C = H_C // KV_C
WINDOW_C = 128
W_C = H_C * HEAD_DIM
KVW_C = KV_C * HEAD_DIM
ROPE_THETA = 10000.0
RMS_EPS = 1e-6
NEG_INF = -1e30
SCALE = HEAD_DIM ** -0.5

LANES = 128
QROWS = 128
A_WIN = (LEFT_CHUNKS_A + 2) * CHUNK
A_TAB = 384
A_ROLLW = A_WIN + QROWS
C_WIN = WINDOW_C + QROWS
STRIP = 32
LOG2E = 1.4426950408889634
SB_BLK = 256
SB_DEAD = 104.0 * LOG2E
VMEM_LIMIT = 48 * 1024 * 1024

F32 = jnp.float32
BF16 = jnp.bfloat16


def _params(n_axes, semantics="parallel"):
    return pltpu.CompilerParams(dimension_semantics=(semantics,) * n_axes,
                                vmem_limit_bytes=VMEM_LIMIT)


def _rms_scale(x):
    return lax.rsqrt(jnp.mean(x * x, axis=-1, keepdims=True) + RMS_EPS)


def _silu(u):
    return u / (1.0 + jnp.exp(-u))


def _low_half(shape):
    return lax.broadcasted_iota(jnp.int32, shape, 1) < HEAD_DIM


def _head_of_pair(x_bf16, low, h):
    xf = x_bf16.astype(F32)
    return jnp.where(low if h == 0 else jnp.logical_not(low), xf, 0.0).astype(BF16)


def _dot_nt(a, b):
    return lax.dot_general(a, b, (((1,), (1,)), ((), ())), preferred_element_type=F32)


def _proj_ab_kernel(x_ref, g_ref, w_ref, qa, ka, va, sga, qb, kb, vb, sgb, kaf, vaf, kbf, vbf,
                    *, tiles_per_seq, keep_a):
    tm = x_ref.shape[0]

    def body(with_tails):
        x = x_ref[...]
        xn = (x * _rms_scale(x) * g_ref[...]).astype(BF16)

        def col(j):
            return jnp.dot(xn, w_ref[:, j * W_A:(j + 1) * W_A], preferred_element_type=F32)

        sga[...] = _silu(col(3)).astype(BF16)
        sgb[...] = _silu(col(7)).astype(BF16)
        u = col(5); kb[...] = u.astype(BF16); kbf[...] = u.T
        u = col(6); vb[...] = u.astype(BF16); vbf[...] = u.T
        u = col(1); ka[...] = u.astype(BF16)
        if with_tails:
            kaf[...] = u[tm - keep_a:, :].T
        u = col(2); va[...] = u.astype(BF16)
        if with_tails:
            vaf[...] = u[tm - keep_a:, :].T
        qa[...] = (col(0) * (SCALE * LOG2E)).astype(BF16)
        qb[...] = (col(4) * (SCALE * LOG2E)).astype(BF16)

    if tiles_per_seq == 1:
        body(True)
    else:
        last = pl.program_id(0) % tiles_per_seq == tiles_per_seq - 1
        pl.when(last)(functools.partial(body, True))
        pl.when(jnp.logical_not(last))(functools.partial(body, False))


def _proj_ab(x2d, g_pre, w_bf16, tm, tiles_per_seq, keep_a):
    T = x2d.shape[0]
    n_tiles = T // tm
    row = lambda w: pl.BlockSpec((tm, w), lambda i: (i, 0))
    const = lambda a: pl.BlockSpec(a.shape, lambda i: (0, 0), pipeline_mode=pl.Buffered(1))
    bshape = jax.ShapeDtypeStruct((T, W_A), BF16)
    groups = n_tiles // tiles_per_seq
    a_spec = pl.BlockSpec((None, W_A, keep_a), lambda i: (i // tiles_per_seq, 0, 0))
    a_shape = jax.ShapeDtypeStruct((groups, W_A, keep_a), F32)
    b_spec = pl.BlockSpec((None, W_B, tm), lambda i: (i // tiles_per_seq, 0, i % tiles_per_seq))
    b_shape = jax.ShapeDtypeStruct((groups, W_B, tiles_per_seq * tm), F32)
    return pl.pallas_call(
        functools.partial(_proj_ab_kernel, tiles_per_seq=tiles_per_seq, keep_a=keep_a),
        grid=(n_tiles,),
        in_specs=[row(D_MODEL), const(g_pre), const(w_bf16)],
        out_specs=[row(W_A)] * 8 + [a_spec, a_spec, b_spec, b_spec],
        out_shape=[bshape] * 8 + [a_shape, a_shape, b_shape, b_shape],
        compiler_params=_params(1, "arbitrary"),
        name="proj_ab",
    )(x2d, g_pre, w_bf16)


def _merge_body(h_refs, w_ref, g_ref, x_ref, y_ref):
    acc = None
    off = 0
    for h in h_refs:
        k = h.shape[1]
        d = jnp.dot(h[...], w_ref[off:off + k, :], preferred_element_type=F32)
        acc = d if acc is None else acc + d
        off += k
    y = x_ref[...] + acc * _rms_scale(acc) * g_ref[...]
    y_ref[...] = y
    return y


def _merge_kernel(*refs, n_parts):
    _merge_body(refs[:n_parts], *refs[n_parts:])


def _merge(parts, w_bf16, g_post, x2d, tm):
    T = x2d.shape[0]
    row = lambda w: pl.BlockSpec((tm, w), lambda i: (i, 0))
    full = lambda a: pl.BlockSpec(a.shape, lambda i: (0, 0))
    return pl.pallas_call(
        functools.partial(_merge_kernel, n_parts=len(parts)),
        grid=(T // tm,),
        in_specs=[row(p.shape[1]) for p in parts] + [full(w_bf16), full(g_post), row(D_MODEL)],
        out_specs=row(D_MODEL),
        out_shape=jax.ShapeDtypeStruct((T, D_MODEL), F32),
        compiler_params=_params(1),
        name="merge",
    )(*parts, w_bf16, g_post, x2d)


def _bias_kernel(tab_ref, out_ref, *, n_valid):
    m = lax.broadcasted_iota(jnp.int32, (A_TAB, A_ROLLW), 1)
    t = lax.broadcasted_iota(jnp.int32, (A_TAB, A_ROLLW), 0)
    idx = jnp.where(m <= A_WIN, jnp.clip(A_WIN - QROWS - m, -REL_CLIP_A, REL_CLIP_A) + REL_CLIP_A,
                    2 * REL_CLIP_A)
    onehot = jnp.where(t == idx, 1.0, 0.0).astype(BF16)
    tab = tab_ref[...]
    hi = tab.astype(BF16)
    r1 = tab - hi.astype(F32)
    mid = r1.astype(BF16)
    lo = (r1 - mid.astype(F32)).astype(BF16)
    src = (jnp.dot(hi, onehot, preferred_element_type=F32)
           + jnp.dot(mid, onehot, preferred_element_type=F32)
           + jnp.dot(lo, onehot, preferred_element_type=F32))
    i = lax.broadcasted_iota(jnp.int32, (QROWS, A_WIN), 0)
    j = lax.broadcasted_iota(jnp.int32, (QROWS, A_WIN), 1)
    qc = LEFT_CHUNKS_A + i // CHUNK
    kc = j // CHUNK
    valid = (kc <= qc) & (kc >= qc - LEFT_CHUNKS_A) & (j < n_valid)
    for h in range(H_A):
        row = jnp.broadcast_to(src[h:h + 1, :], (QROWS, A_ROLLW))
        toep = pltpu.roll(row, 0, 1, stride=1, stride_axis=0)
        out_ref[h * QROWS:(h + 1) * QROWS, :] = jnp.where(valid, toep[:, :A_WIN] * LOG2E, NEG_INF)


def _bias_table(table, n_valid):
    tab = jnp.pad(table, ((0, 0), (0, A_TAB - N_REL_A)))
    return pl.pallas_call(
        functools.partial(_bias_kernel, n_valid=n_valid),
        out_shape=jax.ShapeDtypeStruct((H_A * QROWS, A_WIN), F32),
        name="a_bias",
    )(tab)


def _pipeline(n_units, n_static, period, stages):
    depth = len(stages)

    def tick(t, tmod):
        for lag in reversed(range(depth)):
            u = t - lag
            if not isinstance(u, int) or 0 <= u < n_units:
                stages[lag](u, (tmod - lag) % period)

    first = n_static + depth - 1
    for t in range(min(first, n_units + depth - 1)):
        tick(t, t % period)
    groups = max(n_units - first, 0) // period

    def body(g, c):
        t = first + period * g
        for d in range(period):
            tick(t + d, (first + d) % period)
        return c

    if groups:
        lax.fori_loop(0, groups, body, 0)
    for t in range(max(first, first + period * groups), n_units + depth - 1):
        tick(t, t % period)


def _softmax_strips(s_scr, p_scr, width):
    for r in range(0, s_scr.shape[0], STRIP):
        x = s_scr[r:r + STRIP, :width]
        p_scr[r:r + STRIP, :width] = jnp.exp2(x - jnp.max(x, axis=-1, keepdims=True)).astype(BF16)


def _attn_a_stages(q_ref, k_ref, v_ref, sg_ref, bias_ref, o_ref, s_scr, p_scr, geom):
    low = _low_half((QROWS, LANES))

    def qk(m, slot):
        q0, k0, width, bcol0 = geom(m)
        qm = q_ref[pl.ds(q0, QROWS), :]
        q2 = jnp.concatenate([_head_of_pair(qm, low, 0), _head_of_pair(qm, low, 1)], axis=0)
        s_scr[slot, :, :width] = _dot_nt(q2, k_ref[pl.ds(k0, width), :]) + bias_ref[:, bcol0:bcol0 + width]

    def sm(m, slot):
        _softmax_strips(s_scr.at[slot], p_scr.at[slot], geom(m)[2])

    def pv(m, slot):
        q0, k0, width, _ = geom(m)
        vw = v_ref[pl.ds(k0, width), :]
        o2 = jnp.dot(p_scr[slot, :, :width], jnp.concatenate([vw, jnp.ones_like(vw)], axis=1),
                     preferred_element_type=F32)
        top, bot = o2[:QROWS], o2[QROWS:]
        o = jnp.where(low, top[:, :LANES] / top[:, LANES:], bot[:, :LANES] / bot[:, LANES:])
        o_ref[pl.ds(q0, QROWS), :] = (o * sg_ref[pl.ds(q0, QROWS), :].astype(F32)).astype(BF16)

    return qk, sm, pv


def _attn_a_kernel(q_ref, k_ref, v_ref, sg_ref, bias_ref, o_ref, s_scr, p_scr, *, n_steps):
    lead = A_WIN // QROWS - 1

    def geom(m):
        if isinstance(m, int) and m < lead:
            return m * QROWS, 0, (m + 1) * QROWS, (lead - m) * QROWS
        if isinstance(m, int):
            return m * QROWS, (m - lead) * QROWS, A_WIN, 0
        return pl.multiple_of(m * QROWS, QROWS), pl.multiple_of((m - lead) * QROWS, QROWS), A_WIN, 0

    _pipeline(n_steps, lead, 2, _attn_a_stages(q_ref, k_ref, v_ref, sg_ref, bias_ref, o_ref, s_scr, p_scr, geom))


def _attn_a_window_kernel(q_ref, k_ref, v_ref, sg_ref, bias_ref, o_ref, s_scr, p_scr):
    _pipeline(1, 1, 2, _attn_a_stages(q_ref, k_ref, v_ref, sg_ref, bias_ref, o_ref, s_scr, p_scr,
                                      lambda m: (0, 0, A_WIN, 0)))


def _attn_a(q, k, v, sg, bias, window):
    B, Sq, _ = q.shape
    Sk = k.shape[1]
    qspec = pl.BlockSpec((None, Sq, LANES), lambda p, b: (b, 0, p))
    kspec = pl.BlockSpec((None, Sk, LANES), lambda p, b: (b, 0, p))
    bspec = pl.BlockSpec((2 * QROWS, A_WIN), lambda p, b: (p, 0))
    kern = _attn_a_window_kernel if window else functools.partial(_attn_a_kernel, n_steps=Sq // QROWS)
    return pl.pallas_call(
        kern,
        grid=(W_A // LANES, B),
        in_specs=[qspec, kspec, kspec, qspec, bspec],
        out_specs=qspec,
        out_shape=jax.ShapeDtypeStruct(q.shape, BF16),
        scratch_shapes=[pltpu.VMEM((2, 2 * QROWS, A_WIN), F32), pltpu.VMEM((2, 2 * QROWS, A_WIN), BF16)],
        compiler_params=_params(2),
        name="attn_a_window" if window else "attn_a",
    )(q, k, v, sg, bias)


def _sb_stages(scr, unit, past_transposed=False):
    z_scr, h_scr, a_scr, rs_scr = scr
    rows = 2 * SB_BLK
    low = _low_half((SB_BLK, LANES))
    ri = lax.broadcasted_iota(jnp.int32, (STRIP, SB_BLK), 0)
    ci = lax.broadcasted_iota(jnp.int32, (STRIP, SB_BLK), 1)
    r = lax.broadcasted_iota(jnp.int32, (SB_BLK, SB_BLK), 0)
    c = lax.broadcasted_iota(jnp.int32, (SB_BLK, SB_BLK), 1)
    tri = jnp.where(r > c, 1.0, 0.0).astype(BF16)

    def scores(u, us):
        q_ref, q0, k_ref, _, k0, diag = unit(u, us)[:6]
        qm = q_ref[pl.ds(q0, SB_BLK), :]
        q2 = jnp.concatenate([_head_of_pair(qm, low, 0), _head_of_pair(qm, low, 1)], axis=0)
        if past_transposed and not diag:
            z_scr[us % 4] = jnp.dot(q2, k_ref[:, pl.ds(k0, SB_BLK)].astype(BF16), preferred_element_type=F32)
        else:
            z_scr[us % 4] = _dot_nt(q2, k_ref[pl.ds(k0, SB_BLK), :].astype(BF16))

    def terms(u, us):
        diag = unit(u, us)[5]
        zs, hs, rs = z_scr.at[us % 4], h_scr.at[us % 2], rs_scr.at[us % 2]
        for r0 in range(0, rows, STRIP):
            z = zs[r0:r0 + STRIP, :]
            sp = jnp.maximum(z, 0.0) + jnp.log2(1.0 + jnp.exp2(-jnp.abs(z)))
            zs[r0:r0 + STRIP, :] = z - sp
            if diag:
                sp = jnp.where(ci < ri + r0 % SB_BLK, sp, 0.0)
            hs[r0:r0 + STRIP, :] = sp.astype(BF16)
            rs[r0:r0 + STRIP, :] = jnp.broadcast_to(jnp.sum(sp, axis=-1, keepdims=True), (STRIP, LANES))

    def weights(u, us):
        diag, dec = unit(u, us)[5:7]
        zs, rs, a_s = z_scr.at[us % 4], rs_scr.at[us % 2], a_scr.at[us % 2]
        inner = jnp.dot(h_scr[us % 2], tri, preferred_element_type=F32)
        for r0 in range(0, rows, STRIP):
            x = zs[r0:r0 + STRIP, :] - inner[r0:r0 + STRIP, :]
            if diag:
                dec[r0:r0 + STRIP, :] = rs[r0:r0 + STRIP, :]
            else:
                d = dec[r0:r0 + STRIP, :]
                x = x - jnp.concatenate([d, d], axis=1)
                dec[r0:r0 + STRIP, :] = d + rs[r0:r0 + STRIP, :]
            a = jnp.exp2(x)
            if diag:
                a = jnp.where(ci < ri + r0 % SB_BLK, a, 0.0)
            a_s[r0:r0 + STRIP, :] = a.astype(BF16)

    def values(u, us):
        _, _, _, v_ref, k0, diag, _, acc = unit(u, us)[:8]
        if past_transposed and not diag:
            pv = _dot_nt(a_scr[us % 2], v_ref[:, pl.ds(k0, SB_BLK)].astype(BF16))
        else:
            pv = jnp.dot(a_scr[us % 2], v_ref[pl.ds(k0, SB_BLK), :].astype(BF16), preferred_element_type=F32)
        if diag:
            acc[...] = pv
        else:
            acc[...] += pv
        if len(unit(u, us)) > 8:
            unit(u, us)[8]()

    return scores, terms, weights, values


def _sb_live(dec):
    return (jnp.min(dec[...]) < SB_DEAD).astype(jnp.int32)


def _sb_deeper(stages, j0, dec):
    def body(c):
        for stage in stages:
            stage(c[0], 0)
        return c[0] - 1, _sb_live(dec)

    lax.while_loop(lambda c: (c[0] >= 0) & (c[1] > 0), body, (jnp.asarray(j0, jnp.int32), jnp.int32(1)))


def _sb_scratch(n_qblocks):
    rows = 2 * SB_BLK
    return [pltpu.VMEM((4, rows, SB_BLK), F32), pltpu.VMEM((2, rows, SB_BLK), BF16),
            pltpu.VMEM((2, rows, SB_BLK), BF16), pltpu.VMEM((2, rows, LANES), F32),
            pltpu.VMEM((n_qblocks, rows, LANES), F32), pltpu.VMEM((n_qblocks, rows, LANES), F32),
            pltpu.SMEM((n_qblocks,), jnp.int32)]


def _sb_output(acc, sg):
    low = _low_half((SB_BLK, LANES))
    return (jnp.where(low, acc[:SB_BLK, :], acc[SB_BLK:, :]) * sg.astype(F32)).astype(BF16)


def _attn_b_kernel(q_ref, k_ref, v_ref, sg_ref, o_ref, z_scr, h_scr, a_scr, rs_scr, dec_all, acc_all, live):
    n_q = q_ref.shape[0] // SB_BLK
    scr = (z_scr, h_scr, a_scr, rs_scr)
    blk = lambda i: i * SB_BLK if isinstance(i, int) else pl.multiple_of(i * SB_BLK, SB_BLK)

    def emit(i):
        rows = pl.ds(blk(i), SB_BLK)
        o_ref[rows, :] = _sb_output(acc_all[i], sg_ref[rows, :])

    def unit(u, us):
        i = (u + 1) // 2
        diag = u % 2 == 1 or u == 0 if isinstance(u, int) else us % 2 == 1
        base = (q_ref, blk(i), k_ref, v_ref, blk(i if diag else i - 1), diag, dec_all.at[i], acc_all.at[i])
        if diag and not (isinstance(u, int) and u == 0):
            return base

        def done():
            emit(i)
            live[i] = _sb_live(dec_all.at[i])

        return base + (done,)

    _pipeline(2 * n_q - 1, 1, 4, _sb_stages(scr, unit))

    def finish(i, c):
        @pl.when(live[i] > 0)
        def _():
            unit_at = lambda j, us: (q_ref, blk(i), k_ref, v_ref, blk(j), False, dec_all.at[i], acc_all.at[i])
            _sb_deeper(_sb_stages(scr, unit_at), i - 2, dec_all.at[i])
            emit(i)

        return c

    lax.fori_loop(2, n_q, finish, 0)


def _attn_b_cached_kernel(q_ref, kd_ref, vd_ref, kp_ref, vp_ref, sg_ref, o_ref,
                          z_scr, h_scr, a_scr, rs_scr, dec_all, acc_all, live):
    n_past = kp_ref.shape[1] // SB_BLK
    scr = (z_scr, h_scr, a_scr, rs_scr)
    state = (dec_all.at[0], acc_all.at[0])

    def emit():
        o_ref[...] = _sb_output(acc_all[0], sg_ref[...])

    def done():
        emit()
        live[0] = _sb_live(dec_all.at[0])

    def unit(u, us):
        if u == 0:
            return (q_ref, 0, kd_ref, vd_ref, 0, True) + state
        return (q_ref, 0, kp_ref, vp_ref, (n_past - 1) * SB_BLK, False) + state + (done,)

    _pipeline(2, 2, 4, _sb_stages(scr, unit, past_transposed=True))

    @pl.when(live[0] > 0)
    def _():
        unit_at = lambda j, us: (q_ref, 0, kp_ref, vp_ref, pl.multiple_of(j * SB_BLK, SB_BLK), False) + state
        _sb_deeper(_sb_stages(scr, unit_at, past_transposed=True), n_past - 2, dec_all.at[0])
        emit()


def _attn_b(q, k, v, sg):
    B, S, _ = q.shape
    spec = pl.BlockSpec((None, S, LANES), lambda p, b: (b, 0, p))
    return pl.pallas_call(
        _attn_b_kernel,
        grid=(W_B // LANES, B),
        in_specs=[spec] * 4,
        out_specs=spec,
        out_shape=jax.ShapeDtypeStruct(q.shape, BF16),
        scratch_shapes=_sb_scratch(S // SB_BLK),
        compiler_params=_params(2),
        name="attn_b",
    )(q, k, v, sg)


def _attn_b_cached(q, kd, vd, kp, vp, sg):
    B = q.shape[0]
    P = kp.shape[2]
    spec = pl.BlockSpec((None, SB_BLK, LANES), lambda p, b: (b, 0, p))
    pspec = pl.BlockSpec((None, LANES, P), lambda p, b: (b, p, 0))
    return pl.pallas_call(
        _attn_b_cached_kernel,
        grid=(W_B // LANES, B),
        in_specs=[spec, spec, spec, pspec, pspec, spec],
        out_specs=spec,
        out_shape=jax.ShapeDtypeStruct(q.shape, BF16),
        scratch_shapes=_sb_scratch(1),
        compiler_params=_params(2),
        name="attn_b_cached",
    )(q, kd, vd, kp, vp, sg)


def _rope_table_kernel(inv_ref, cos_ref, sin_ref, *, pos0, period):
    shape = cos_ref.shape
    row = lax.broadcasted_iota(jnp.int32, shape, 0)
    lane = lax.broadcasted_iota(jnp.int32, shape, 1)
    ang = (pos0 + row % period).astype(F32) * inv_ref[...]
    cos_ref[...] = jnp.cos(ang)
    s = jnp.sin(ang)
    sin_ref[...] = jnp.where(lane % HEAD_DIM < HEAD_DIM // 2, -s, s)


def _rope_tables(n_rows, pos0, period):
    half = HEAD_DIM // 2
    inv = ROPE_THETA ** (-jnp.arange(half, dtype=F32) * (2.0 / HEAD_DIM))
    inv = jnp.tile(inv, LANES // half)[None, :]
    shape = jax.ShapeDtypeStruct((n_rows, LANES), F32)
    return pl.pallas_call(
        functools.partial(_rope_table_kernel, pos0=pos0, period=period),
        out_shape=[shape, shape],
        name="rope_table",
    )(inv)


def _proj_c_kernel(*refs, n_parts):
    if n_parts:
        w_out, g_post, x_ref, g_ref, w_ref, cos_ref, sin_ref, y_ref = refs[n_parts:n_parts + 8]
        x = _merge_body(refs[:n_parts], w_out, g_post, x_ref, y_ref)
        _proj_c_body(x, g_ref, w_ref, cos_ref, sin_ref, *refs[n_parts + 8:])
    else:
        _proj_c_body(refs[0][...], *refs[1:])


def _proj_c_body(x, g_ref, w_ref, cos_ref, sin_ref, q_o, k_o, v_o, sg_o, kf_o, vf_o):
    xn = (x * _rms_scale(x) * g_ref[...]).astype(BF16)
    tm = x.shape[0]
    keep = kf_o.shape[0]

    def keep_tail(o_ref, u):
        o_ref[...] = u[tm - keep:, :]

    cos = cos_ref[...]
    sin = sin_ref[...]
    first = lax.broadcasted_iota(jnp.int32, cos.shape, 1) % HEAD_DIM < HEAD_DIM // 2

    def rope(u):
        partner = jnp.where(first, pltpu.roll(u, LANES - HEAD_DIM // 2, 1), pltpu.roll(u, HEAD_DIM // 2, 1))
        return u * cos + partner * sin

    blk = 4 * LANES
    for c0 in range(0, W_C, blk):
        u = jnp.dot(xn, w_ref[:, c0:c0 + blk], preferred_element_type=F32)
        for j in range(0, blk, LANES):
            q_o[:, c0 + j:c0 + j + LANES] = (rope(u[:, j:j + LANES]) * (SCALE * LOG2E)).astype(BF16)
    u = jnp.dot(xn, w_ref[:, W_C:W_C + KVW_C], preferred_element_type=F32)
    r = jnp.concatenate([rope(u[:, j:j + LANES]) for j in range(0, KVW_C, LANES)], axis=1)
    k_o[...] = r.astype(BF16)
    keep_tail(kf_o, r)
    u = jnp.dot(xn, w_ref[:, W_C + KVW_C:W_C + 2 * KVW_C], preferred_element_type=F32)
    v_o[...] = u.astype(BF16)
    keep_tail(vf_o, u)
    g0 = W_C + 2 * KVW_C
    for c0 in range(0, W_C, blk):
        u = jnp.dot(xn, w_ref[:, g0 + c0:g0 + c0 + blk], preferred_element_type=F32)
        sg_o[:, c0:c0 + blk] = _silu(u).astype(BF16)


def _proj_c(x2d, g_pre, w_bf16, cos, sin, tm, tiles_per_seq, keep, merge=None):
    parts, w_out, g_post, x_res = merge if merge is not None else ((), None, None, None)
    T = (x_res if merge is not None else x2d).shape[0]
    n_tiles = T // tm
    nt = cos.shape[0] // tm
    row = lambda w: pl.BlockSpec((tm, w), lambda i: (i, 0))
    const = lambda a: pl.BlockSpec(a.shape, lambda i: (0, 0), pipeline_mode=pl.Buffered(1))
    tab = pl.BlockSpec((tm, LANES), lambda i: (i % nt, 0))
    tail = pl.BlockSpec((keep, KVW_C), lambda i: (i // tiles_per_seq, 0))
    sd = jax.ShapeDtypeStruct
    tail_shape = sd((n_tiles // tiles_per_seq * keep, KVW_C), F32)
    if merge is not None:
        operands = [*parts, w_out, g_post, x_res]
        in_specs = [row(p.shape[1]) for p in parts] + [const(w_out), const(g_post), row(D_MODEL)]
        out_specs, out_shape = [row(D_MODEL)], [sd((T, D_MODEL), F32)]
    else:
        operands, in_specs, out_specs, out_shape = [x2d], [row(D_MODEL)], [], []
    return pl.pallas_call(
        functools.partial(_proj_c_kernel, n_parts=len(parts)),
        grid=(n_tiles,),
        in_specs=in_specs + [const(g_pre), const(w_bf16), tab, tab],
        out_specs=out_specs + [row(W_C), row(KVW_C), row(KVW_C), row(W_C), tail, tail],
        out_shape=out_shape + [sd((T, W_C), BF16), sd((T, KVW_C), BF16), sd((T, KVW_C), BF16),
                               sd((T, W_C), BF16), tail_shape, tail_shape],
        compiler_params=_params(1, "arbitrary"),
        name="merge_proj_c" if merge is not None else "proj_c",
    )(*operands, g_pre, w_bf16, cos, sin)


def _attn_c_stages(sink_ref, q_ref, k_ref, v_ref, sg_ref, o_ref, s_scr, p_scr, e_scr, geom, n_valid):
    low = _low_half((QROWS, LANES))
    blocks = [(hh, g) for hh in range(2) for g in range(G_C)]

    def qk(m, slot):
        q0, k0, width, mcol0 = geom(m)
        i = lax.broadcasted_iota(jnp.int32, (QROWS, width), 0) // CHUNK
        j = lax.broadcasted_iota(jnp.int32, (QROWS, width), 1)
        jc = (j + mcol0) // CHUNK
        valid = (jc <= i + WINDOW_C // CHUNK) & (jc >= i)
        if n_valid is not None:
            valid = valid & (j < n_valid)
        mask = jnp.where(valid, 0.0, NEG_INF)
        rows = pl.ds(q0, QROWS)
        q2 = jnp.concatenate([_head_of_pair(q_ref[rows, g * LANES:(g + 1) * LANES], low, hh)
                              for hh, g in blocks], axis=0)
        z = _dot_nt(q2, k_ref[pl.ds(k0, width), :])
        for bi in range(len(blocks)):
            s_scr[slot, bi * QROWS:(bi + 1) * QROWS, :width] = z[bi * QROWS:(bi + 1) * QROWS] + mask

    def sm(m, slot):
        width = geom(m)[2]
        pair = pl.program_id(0)
        for bi, (hh, g) in enumerate(blocks):
            sink = sink_ref[(2 * pair + hh) * G_C + g] * LOG2E
            for r in range(bi * QROWS, (bi + 1) * QROWS, STRIP):
                x = s_scr[slot, r:r + STRIP, :width]
                mx = jnp.maximum(jnp.max(x, axis=-1, keepdims=True), sink)
                p_scr[slot, r:r + STRIP, :width] = jnp.exp2(x - mx).astype(BF16)
                e_scr[slot, r:r + STRIP, :] = jnp.broadcast_to(jnp.exp2(sink - mx), (STRIP, LANES))

    def pv(m, slot):
        q0, k0, width, _ = geom(m)
        rows = pl.ds(q0, QROWS)
        vw = v_ref[pl.ds(k0, width), :]
        o2 = jnp.dot(p_scr[slot, :, :width], jnp.concatenate([vw, jnp.ones_like(vw)], axis=1),
                     preferred_element_type=F32)
        for g in range(G_C):
            t0, b0 = g * QROWS, (G_C + g) * QROWS
            top = o2[t0:t0 + QROWS, :LANES] / (o2[t0:t0 + QROWS, LANES:] + e_scr[slot, t0:t0 + QROWS, :])
            bot = o2[b0:b0 + QROWS, :LANES] / (o2[b0:b0 + QROWS, LANES:] + e_scr[slot, b0:b0 + QROWS, :])
            cols = slice(g * LANES, (g + 1) * LANES)
            o_ref[rows, cols] = (jnp.where(low, top, bot) * sg_ref[rows, cols].astype(F32)).astype(BF16)

    return qk, sm, pv


def _attn_c_kernel(sink_ref, q_ref, k_ref, v_ref, sg_ref, o_ref, s_scr, p_scr, e_scr, *, n_steps):
    def geom(m):
        if isinstance(m, int) and m == 0:
            return 0, 0, QROWS, C_WIN - QROWS
        if isinstance(m, int):
            return m * QROWS, (m - 1) * QROWS, C_WIN, 0
        return pl.multiple_of(m * QROWS, QROWS), pl.multiple_of((m - 1) * QROWS, QROWS), C_WIN, 0

    _pipeline(n_steps, 1, 2, _attn_c_stages(sink_ref, q_ref, k_ref, v_ref, sg_ref, o_ref,
                                            s_scr, p_scr, e_scr, geom, None))


def _attn_c_window_kernel(sink_ref, q_ref, k_ref, v_ref, sg_ref, o_ref, s_scr, p_scr, e_scr, *, n_valid):
    _pipeline(1, 1, 2, _attn_c_stages(sink_ref, q_ref, k_ref, v_ref, sg_ref, o_ref, s_scr, p_scr, e_scr,
                                      lambda m: (0, 0, C_WIN, 0), n_valid))


def _attn_c(q, k, v, sg, sinks, n_valid=None):
    B, Sq, _ = q.shape
    Sk = k.shape[1]
    qw = W_C // (KVW_C // LANES)
    qspec = pl.BlockSpec((None, Sq, qw), lambda p, b: (b, 0, p))
    kspec = pl.BlockSpec((None, Sk, LANES), lambda p, b: (b, 0, p))
    sspec = pl.BlockSpec(memory_space=pltpu.SMEM)
    if n_valid is None:
        kern = functools.partial(_attn_c_kernel, n_steps=Sq // QROWS)
    else:
        kern = functools.partial(_attn_c_window_kernel, n_valid=n_valid)
    stacked = 2 * G_C * QROWS
    return pl.pallas_call(
        kern,
        grid=(KVW_C // LANES, B),
        in_specs=[sspec, qspec, kspec, kspec, qspec],
        out_specs=qspec,
        out_shape=jax.ShapeDtypeStruct(q.shape, BF16),
        scratch_shapes=[pltpu.VMEM((2, stacked, C_WIN), F32), pltpu.VMEM((2, stacked, C_WIN), BF16),
                        pltpu.VMEM((2, stacked, LANES), F32)],
        compiler_params=_params(2),
        name="attn_c" if n_valid is None else "attn_c_window",
    )(sinks, q, k, v, sg)


def _c_head_order():
    heads = [(2 * P + hh) * G_C + g for P in range(KV_C // 2) for g in range(G_C) for hh in range(2)]
    return jnp.asarray([h * HEAD_DIM + d for h in heads for d in range(HEAD_DIM)], jnp.int32)


def _pad_rows(x, n):
    return jnp.pad(x, ((0, 0), (0, n - x.shape[1]), (0, 0)))


def _heads(x2d, B, h):
    return x2d.reshape(B, x2d.shape[0] // B, h, HEAD_DIM)


def _flush(act):
    if not isinstance(act, tuple):
        return act
    merge, B, S, tm = act
    return _merge(*merge, tm).reshape(B, S, D_MODEL)


def _ab_layer(x, cache, g_pre, w_in, w_out, g_post, table):
    x = _flush(x)
    B, S, _ = x.shape
    T = B * S
    x2d = x.reshape(T, D_MODEL)
    r3 = lambda t: t.reshape(B, S, -1)

    def cached(t, rows):
        g, w, c = t.shape
        t = t.reshape(g, w // HEAD_DIM, HEAD_DIM, c // rows, rows)
        return jnp.transpose(t, (0, 3, 4, 1, 2)).reshape(B, rows, w // HEAD_DIM, HEAD_DIM)

    if cache is None:
        tm = min(S, 512)
        la = min(LEFT_CHUNKS_A * CHUNK, S)
        assert S % tm == 0 and la <= tm
        proj = _proj_ab(x2d, g_pre[None, :], w_in.astype(BF16), tm, S // tm, la)
        qa, ka, va, sga, qb, kb, vb, sgb = proj[:8]
        oa = _attn_a(r3(qa), r3(ka), r3(va), r3(sga), _bias_table(table, A_WIN), window=False)
        ob = _attn_b(r3(qb), r3(kb), r3(vb), r3(sgb))
    else:
        ca_k, ca_v, cb_k, cb_v = cache
        la = ca_k.shape[1]
        assert la == LEFT_CHUNKS_A * CHUNK and S <= CHUNK and cb_k.shape[1] % SB_BLK == 0
        keys_minor = lambda c: jnp.transpose(c, (0, 2, 3, 1)).reshape(B, c.shape[2] * HEAD_DIM, c.shape[1])
        tm = T
        proj = _proj_ab(x2d, g_pre[None, :], w_in.astype(BF16), tm, 1, tm)
        qa, ka, va, sga, qb, kb, vb, sgb = proj[:8]
        flat = lambda c: c.reshape(B, c.shape[1], -1)
        win_k = _pad_rows(jnp.concatenate([flat(ca_k).astype(BF16), r3(ka)], axis=1), A_WIN)
        win_v = _pad_rows(jnp.concatenate([flat(ca_v).astype(BF16), r3(va)], axis=1), A_WIN)
        oa = _attn_a(_pad_rows(r3(qa), QROWS), win_k, win_v, _pad_rows(r3(sga), QROWS),
                     _bias_table(table, la + S), window=True)[:, :S]
        pb = lambda t: _pad_rows(r3(t), SB_BLK)
        ob = _attn_b_cached(pb(qb), pb(kb), pb(vb), keys_minor(cb_k), keys_minor(cb_v), pb(sgb))[:, :S]
    rows = (cached(proj[8], min(la, S)), cached(proj[9], min(la, S)), cached(proj[10], S), cached(proj[11], S))
    merge = ([oa.reshape(T, W_A), ob.reshape(T, W_B)], w_out.astype(BF16), g_post[None, :], x2d)
    return (merge, B, S, tm), rows


def _c_layer(x, cache, past, g_pre, w_in, sinks, w_out, g_post):
    if isinstance(x, tuple):
        merge, B, S, _ = x
        x2d = None
    else:
        merge, (B, S, _) = None, x.shape
        x2d = x.reshape(B * S, D_MODEL)
    T = B * S
    order = _c_head_order()
    g0 = W_C + 2 * KVW_C
    w_perm = jnp.concatenate([w_in[:, :W_C][:, order], w_in[:, W_C:g0], w_in[:, g0:][:, order]], axis=1)
    r3 = lambda t: t.reshape(B, S, -1)
    if cache is None:
        tm = min(S, 512)
        lc = min(WINDOW_C, S)
        assert S % tm == 0 and lc <= tm
        cos, sin = _rope_tables(S, 0, S)
        proj = _proj_c(x2d, g_pre[None, :], w_perm.astype(BF16), cos, sin, tm, S // tm, lc, merge)
        x2d, (q, k, v, sg, kf, vf) = (proj[0], proj[1:]) if merge is not None else (x2d, proj)
        o = _attn_c(r3(q), r3(k), r3(v), r3(sg), sinks)
    else:
        tm = T
        cos, sin = _rope_tables(T, past, S)
        proj = _proj_c(x2d, g_pre[None, :], w_perm.astype(BF16), cos, sin, tm, 1, tm, merge)
        x2d, (q, k, v, sg, kf, vf) = (proj[0], proj[1:]) if merge is not None else (x2d, proj)
        cc_k, cc_v = cache
        lc = cc_k.shape[1]
        assert lc == WINDOW_C and S <= CHUNK
        flat = lambda c: c.reshape(B, lc, -1).astype(BF16)
        win_k = _pad_rows(jnp.concatenate([flat(cc_k), r3(k)], axis=1), C_WIN)
        win_v = _pad_rows(jnp.concatenate([flat(cc_v), r3(v)], axis=1), C_WIN)
        o = _attn_c(_pad_rows(r3(q), QROWS), win_k, win_v, _pad_rows(r3(sg), QROWS), sinks,
                    n_valid=lc + S)[:, :S]
    rows = (_heads(kf, B, KV_C), _heads(vf, B, KV_C))
    merge = ([o.reshape(T, W_C)], w_out[order, :].astype(BF16), g_post[None, :], x2d)
    return (merge, B, S, tm), rows


def kernel(x_prompt, x_sample, cache_a_k, cache_a_v, cache_b_k, cache_b_v, cache_c_k, cache_c_v,
           ab_norm_pre, ab_w_in, ab_w_out, ab_norm_post, a_rel_bias,
           c_norm_pre, c_w_in, c_sinks, c_w_out, c_norm_post):
    past = cache_b_k.shape[2]
    depth = ab_w_in.shape[0] + c_w_in.shape[0]
    yp, ys = x_prompt, x_sample
    ab_p, ab_s, c_p, c_s = [], [], [], []
    for layer in range(depth):
        i = layer // 2
        if layer % 2 == 0:
            w = (ab_norm_pre[i], ab_w_in[i], ab_w_out[i], ab_norm_post[i], a_rel_bias[i])
            yp, rows = _ab_layer(yp, None, *w)
            ab_p.append(rows)
            ys, rows = _ab_layer(ys, (cache_a_k[i], cache_a_v[i], cache_b_k[i], cache_b_v[i]), *w)
            ab_s.append(rows)
        else:
            w = (c_norm_pre[i], c_w_in[i], c_sinks[i], c_w_out[i], c_norm_post[i])
            yp, rows = _c_layer(yp, None, past, *w)
            c_p.append(rows)
            ys, rows = _c_layer(ys, (cache_c_k[i], cache_c_v[i]), past, *w)
            c_s.append(rows)
    st = lambda rows, j: jnp.stack([r[j] for r in rows])
    return (_flush(yp), _flush(ys),
            st(ab_p, 0), st(ab_p, 1), st(ab_p, 2), st(ab_p, 3), st(c_p, 0), st(c_p, 1),
            st(ab_s, 0), st(ab_s, 1), st(ab_s, 2), st(ab_s, 3), st(c_s, 0), st(c_s, 1))
```

```python
import functools

import jax
import jax.numpy as jnp
from jax import lax
from jax.experimental import pallas as pl
from jax.experimental.pallas import tpu as pltpu

D_MODEL = 1024
CHUNK = 64
HEAD_DIM = 64
H_A = 8
H_B = 8
LEFT_CHUNKS_A = 8
REL_CLIP_A = 128
N_REL_A = 2 * REL_CLIP_A + 1
W_A = H_A * HEAD_DIM
W_B = H_B * HEAD_DIM
H_C = 16
KV_C = 4
G_C = H_C // KV_C
WINDOW_C = 128
W_C = H_C * HEAD_DIM
KVW_C = KV_C * HEAD_DIM
ROPE_THETA = 10000.0
RMS_EPS = 1e-6
NEG_INF = -1e30
SCALE = HEAD_DIM ** -0.5

LANES = 128
QROWS = 128
A_WIN = (LEFT_CHUNKS_A + 2) * CHUNK
A_TAB = 384
A_ROLLW = A_WIN + QROWS
C_WIN = WINDOW_C + QROWS
MERGE_ROWS = 1024
STRIP = 32
LOG2E = 1.4426950408889634
SB_BLK = 256
SB_DEAD = 104.0 * LOG2E
VMEM_LIMIT = 48 * 1024 * 1024

F32 = jnp.float32
BF16 = jnp.bfloat16


def _params(n_axes, semantics="parallel"):
    return pltpu.CompilerParams(dimension_semantics=(semantics,) * n_axes,
                                vmem_limit_bytes=VMEM_LIMIT)


def _rms_scale(x):
    return lax.rsqrt(jnp.mean(x * x, axis=-1, keepdims=True) + RMS_EPS)


def _silu(u):
    return u / (1.0 + jnp.exp(-u))


def _low_half(shape):
    return lax.broadcasted_iota(jnp.int32, shape, 1) < HEAD_DIM


def _head_of_pair(x_bf16, low, h):
    xf = x_bf16.astype(F32)
    return jnp.where(low if h == 0 else jnp.logical_not(low), xf, 0.0).astype(BF16)


def _dot_nt(a, b):
    return lax.dot_general(a, b, (((1,), (1,)), ((), ())), preferred_element_type=F32)


def _proj_ab_kernel(x_ref, g_ref, w_ref, qa, ka, va, sga, qb, kb, vb, sgb, kaf, vaf, kbf, vbf,
                    *, tiles_per_seq, keep_a):
    tm = x_ref.shape[0]

    def body(with_tails):
        x = x_ref[...]
        xn = (x * _rms_scale(x) * g_ref[...]).astype(BF16)

        def col(j):
            return jnp.dot(xn, w_ref[:, j * W_A:(j + 1) * W_A], preferred_element_type=F32)

        sga[...] = _silu(col(3)).astype(BF16)
        sgb[...] = _silu(col(7)).astype(BF16)
        u = col(5); kb[...] = u.astype(BF16); kbf[...] = u.T
        u = col(6); vb[...] = u.astype(BF16); vbf[...] = u.T
        u = col(1); ka[...] = u.astype(BF16)
        if with_tails:
            kaf[...] = u[tm - keep_a:, :].T
        u = col(2); va[...] = u.astype(BF16)
        if with_tails:
            vaf[...] = u[tm - keep_a:, :].T
        qa[...] = (col(0) * (SCALE * LOG2E)).astype(BF16)
        qb[...] = (col(4) * (SCALE * LOG2E)).astype(BF16)

    if tiles_per_seq == 1:
        body(True)
    else:
        last = pl.program_id(0) % tiles_per_seq == tiles_per_seq - 1
        pl.when(last)(functools.partial(body, True))
        pl.when(jnp.logical_not(last))(functools.partial(body, False))


def _proj_ab(x2d, g_pre, w_bf16, tm, tiles_per_seq, keep_a):
    T = x2d.shape[0]
    n_tiles = T // tm
    row = lambda w: pl.BlockSpec((tm, w), lambda i: (i, 0))
    const = lambda a: pl.BlockSpec(a.shape, lambda i: (0, 0), pipeline_mode=pl.Buffered(1))
    bshape = jax.ShapeDtypeStruct((T, W_A), BF16)
    groups = n_tiles // tiles_per_seq
    a_spec = pl.BlockSpec((None, W_A, keep_a), lambda i: (i // tiles_per_seq, 0, 0))
    a_shape = jax.ShapeDtypeStruct((groups, W_A, keep_a), F32)
    b_spec = pl.BlockSpec((None, W_B, tm), lambda i: (i // tiles_per_seq, 0, i % tiles_per_seq))
    b_shape = jax.ShapeDtypeStruct((groups, W_B, tiles_per_seq * tm), F32)
    return pl.pallas_call(
        functools.partial(_proj_ab_kernel, tiles_per_seq=tiles_per_seq, keep_a=keep_a),
        grid=(n_tiles,),
        in_specs=[row(D_MODEL), const(g_pre), const(w_bf16)],
        out_specs=[row(W_A)] * 8 + [a_spec, a_spec, b_spec, b_spec],
        out_shape=[bshape] * 8 + [a_shape, a_shape, b_shape, b_shape],
        compiler_params=_params(1, "arbitrary"),
        name="proj_ab",
    )(x2d, g_pre, w_bf16)


def _merge_body(h_refs, w_ref, g_ref, x_ref, y_ref):
    acc = None
    off = 0
    for h in h_refs:
        k = h.shape[1]
        d = jnp.dot(h[...], w_ref[off:off + k, :], preferred_element_type=F32)
        acc = d if acc is None else acc + d
        off += k
    y = x_ref[...] + acc * _rms_scale(acc) * g_ref[...]
    y_ref[...] = y
    return y


def _merge_kernel(*refs, n_parts):
    _merge_body(refs[:n_parts], *refs[n_parts:])


def _merge(parts, w_bf16, g_post, x2d, tm):
    T = x2d.shape[0]
    row = lambda w: pl.BlockSpec((tm, w), lambda i: (i, 0))
    full = lambda a: pl.BlockSpec(a.shape, lambda i: (0, 0))
    return pl.pallas_call(
        functools.partial(_merge_kernel, n_parts=len(parts)),
        grid=(T // tm,),
        in_specs=[row(p.shape[1]) for p in parts] + [full(w_bf16), full(g_post), row(D_MODEL)],
        out_specs=row(D_MODEL),
        out_shape=jax.ShapeDtypeStruct((T, D_MODEL), F32),
        compiler_params=_params(1),
        name="merge",
    )(*parts, w_bf16, g_post, x2d)


def _bias_kernel(tab_ref, out_ref, *, n_valid):
    m = lax.broadcasted_iota(jnp.int32, (A_TAB, A_ROLLW), 1)
    t = lax.broadcasted_iota(jnp.int32, (A_TAB, A_ROLLW), 0)
    idx = jnp.where(m <= A_WIN, jnp.clip(A_WIN - QROWS - m, -REL_CLIP_A, REL_CLIP_A) + REL_CLIP_A,
                    2 * REL_CLIP_A)
    onehot = jnp.where(t == idx, 1.0, 0.0).astype(BF16)
    tab = tab_ref[...]
    hi = tab.astype(BF16)
    r1 = tab - hi.astype(F32)
    mid = r1.astype(BF16)
    lo = (r1 - mid.astype(F32)).astype(BF16)
    src = (jnp.dot(hi, onehot, preferred_element_type=F32)
           + jnp.dot(mid, onehot, preferred_element_type=F32)
           + jnp.dot(lo, onehot, preferred_element_type=F32))
    i = lax.broadcasted_iota(jnp.int32, (QROWS, A_WIN), 0)
    j = lax.broadcasted_iota(jnp.int32, (QROWS, A_WIN), 1)
    qc = LEFT_CHUNKS_A + i // CHUNK
    kc = j // CHUNK
    valid = (kc <= qc) & (kc >= qc - LEFT_CHUNKS_A) & (j < n_valid)
    for h in range(H_A):
        row = jnp.broadcast_to(src[h:h + 1, :], (QROWS, A_ROLLW))
        toep = pltpu.roll(row, 0, 1, stride=1, stride_axis=0)
        out_ref[h * QROWS:(h + 1) * QROWS, :] = jnp.where(valid, toep[:, :A_WIN] * LOG2E, NEG_INF)


def _bias_table(table, n_valid):
    tab = jnp.pad(table, ((0, 0), (0, A_TAB - N_REL_A)))
    return pl.pallas_call(
        functools.partial(_bias_kernel, n_valid=n_valid),
        out_shape=jax.ShapeDtypeStruct((H_A * QROWS, A_WIN), F32),
        name="a_bias",
    )(tab)


def _pipeline(n_units, n_static, period, stages):
    depth = len(stages)

    def tick(t, tmod):
        for lag in reversed(range(depth)):
            u = t - lag
            if not isinstance(u, int) or 0 <= u < n_units:
                stages[lag](u, (tmod - lag) % period)

    first = n_static + depth - 1
    for t in range(min(first, n_units + depth - 1)):
        tick(t, t % period)
    groups = max(n_units - first, 0) // period

    def body(g, c):
        t = first + period * g
        for d in range(period):
            tick(t + d, (first + d) % period)
        return c

    if groups:
        lax.fori_loop(0, groups, body, 0)
    for t in range(max(first, first + period * groups), n_units + depth - 1):
        tick(t, t % period)


def _softmax_strips(s_scr, p_scr, width):
    for r in range(0, s_scr.shape[0], STRIP):
        x = s_scr[r:r + STRIP, :width]
        p_scr[r:r + STRIP, :width] = jnp.exp2(x - jnp.max(x, axis=-1, keepdims=True)).astype(BF16)


def _attn_a_stages(q_ref, k_ref, v_ref, sg_ref, bias_ref, o_ref, s_scr, p_scr, geom):
    low = _low_half((QROWS, LANES))

    def qk(m, slot):
        q0, k0, width, bcol0 = geom(m)
        qm = q_ref[pl.ds(q0, QROWS), :]
        q2 = jnp.concatenate([_head_of_pair(qm, low, 0), _head_of_pair(qm, low, 1)], axis=0)
        s_scr[slot, :, :width] = _dot_nt(q2, k_ref[pl.ds(k0, width), :]) + bias_ref[:, bcol0:bcol0 + width]

    def sm(m, slot):
        _softmax_strips(s_scr.at[slot], p_scr.at[slot], geom(m)[2])

    def pv(m, slot):
        q0, k0, width, _ = geom(m)
        vw = v_ref[pl.ds(k0, width), :]
        o2 = jnp.dot(p_scr[slot, :, :width], jnp.concatenate([vw, jnp.ones_like(vw)], axis=1),
                     preferred_element_type=F32)
        top, bot = o2[:QROWS], o2[QROWS:]
        o = jnp.where(low, top[:, :LANES] / top[:, LANES:], bot[:, :LANES] / bot[:, LANES:])
        o_ref[pl.ds(q0, QROWS), :] = (o * sg_ref[pl.ds(q0, QROWS), :].astype(F32)).astype(BF16)

    return qk, sm, pv


def _attn_a_kernel(q_ref, k_ref, v_ref, sg_ref, bias_ref, o_ref, s_scr, p_scr, *, n_steps):
    lead = A_WIN // QROWS - 1

    def geom(m):
        if isinstance(m, int) and m < lead:
            return m * QROWS, 0, (m + 1) * QROWS, (lead - m) * QROWS
        if isinstance(m, int):
            return m * QROWS, (m - lead) * QROWS, A_WIN, 0
        return pl.multiple_of(m * QROWS, QROWS), pl.multiple_of((m - lead) * QROWS, QROWS), A_WIN, 0

    _pipeline(n_steps, lead, 2, _attn_a_stages(q_ref, k_ref, v_ref, sg_ref, bias_ref, o_ref, s_scr, p_scr, geom))


def _attn_a_window_kernel(q_ref, kt_ref, vt_ref, sg_ref, bias_ref, o_ref, s_scr, p_scr):
    S = q_ref.shape[0]
    low = _low_half((S, LANES))
    qm = q_ref[...]
    q2 = jnp.concatenate([_head_of_pair(qm, low, 0), _head_of_pair(qm, low, 1)], axis=0)
    bias = jnp.concatenate([bias_ref[:S, :], bias_ref[QROWS:QROWS + S, :]], axis=0)
    s_view, p_view = s_scr.at[0, pl.ds(0, 2 * S)], p_scr.at[0, pl.ds(0, 2 * S)]
    s_view[...] = jnp.dot(q2, kt_ref[...], preferred_element_type=F32) + bias
    _softmax_strips(s_view, p_view, A_WIN)
    vt = vt_ref[...]
    o2 = _dot_nt(p_view[...], jnp.concatenate([vt, jnp.ones_like(vt)], axis=0))
    o = jnp.where(low, o2[:S, :LANES] / o2[:S, LANES:], o2[S:, :LANES] / o2[S:, LANES:])
    o_ref[...] = (o * sg_ref[...].astype(F32)).astype(BF16)


def _attn_a(q, k, v, sg, bias, window):
    B, Sq, _ = q.shape
    qspec = pl.BlockSpec((None, Sq, LANES), lambda p, b: (b, 0, p))
    if window:
        assert Sq % STRIP == 0 and Sq <= CHUNK and k.shape[2] == A_WIN
        kspec = pl.BlockSpec((None, LANES, A_WIN), lambda p, b: (b, p, 0))
    else:
        kspec = pl.BlockSpec((None, Sq, LANES), lambda p, b: (b, 0, p))
    bspec = pl.BlockSpec((2 * QROWS, A_WIN), lambda p, b: (p, 0))
    kern = _attn_a_window_kernel if window else functools.partial(_attn_a_kernel, n_steps=Sq // QROWS)
    return pl.pallas_call(
        kern,
        grid=(W_A // LANES, B),
        in_specs=[qspec, kspec, kspec, qspec, bspec],
        out_specs=qspec,
        out_shape=jax.ShapeDtypeStruct(q.shape, BF16),
        scratch_shapes=[pltpu.VMEM((2, 2 * QROWS, A_WIN), F32), pltpu.VMEM((2, 2 * QROWS, A_WIN), BF16)],
        compiler_params=_params(2),
        name="attn_a_window" if window else "attn_a",
    )(q, k, v, sg, bias)


def _sb_stages(scr, unit, past_transposed=False, qrows=SB_BLK, diag_w=SB_BLK):
    z_scr, h_scr, a_scr, rs_scr = scr
    rows = 2 * qrows
    low = _low_half((qrows, LANES))
    before = {w: lax.broadcasted_iota(jnp.int32, (STRIP, w), 1) - lax.broadcasted_iota(jnp.int32, (STRIP, w), 0)
              for w in {LANES, SB_BLK, diag_w}}
    tri = {}
    for kw in {SB_BLK, diag_w}:
        r = lax.broadcasted_iota(jnp.int32, (kw, kw), 0)
        c = lax.broadcasted_iota(jnp.int32, (kw, kw), 1)
        tri[kw] = jnp.where(r > c, 1.0, 0.0).astype(BF16)
    keys = lambda diag: diag_w if diag else SB_BLK

    def scores(u, us):
        q_ref, q0, k_ref, _, k0, diag = unit(u, us)[:6]
        qm = q_ref[pl.ds(q0, qrows), :]
        q2 = jnp.concatenate([_head_of_pair(qm, low, 0), _head_of_pair(qm, low, 1)], axis=0)
        kw = keys(diag)
        if past_transposed and not diag:
            z_scr[us % 4, :, :kw] = jnp.dot(q2, k_ref[:, pl.ds(k0, kw)].astype(BF16), preferred_element_type=F32)
        else:
            z_scr[us % 4, :, :kw] = _dot_nt(q2, k_ref[pl.ds(k0, kw), :].astype(BF16))

    def width(diag, r0):
        return LANES if diag and r0 % qrows + STRIP <= LANES else keys(diag)

    def terms(u, us):
        diag = unit(u, us)[5]
        zs, hs, rs = z_scr.at[us % 4], h_scr.at[us % 2], rs_scr.at[us % 2]
        for r0 in range(0, rows, STRIP):
            w = width(diag, r0)
            z = zs[r0:r0 + STRIP, :w]
            sp = jnp.maximum(z, 0.0) + jnp.log2(1.0 + jnp.exp2(-jnp.abs(z)))
            zs[r0:r0 + STRIP, :w] = z - sp
            if diag:
                sp = jnp.where(before[w] < r0 % qrows, sp, 0.0)
            hs[r0:r0 + STRIP, :w] = sp.astype(BF16)
            if w < keys(diag):
                hs[r0:r0 + STRIP, w:keys(diag)] = jnp.zeros((STRIP, keys(diag) - w), BF16)
            rs[r0:r0 + STRIP, :] = jnp.broadcast_to(jnp.sum(sp, axis=-1, keepdims=True), (STRIP, LANES))

    def weights(u, us):
        diag, dec = unit(u, us)[5:7]
        zs, rs, a_s = z_scr.at[us % 4], rs_scr.at[us % 2], a_scr.at[us % 2]
        kw = keys(diag)
        inner = jnp.dot(h_scr[us % 2, :, :kw], tri[kw], preferred_element_type=F32)
        for r0 in range(0, rows, STRIP):
            w = width(diag, r0)
            x = zs[r0:r0 + STRIP, :w] - inner[r0:r0 + STRIP, :w]
            if diag:
                dec[r0:r0 + STRIP, :] = rs[r0:r0 + STRIP, :]
            else:
                d = dec[r0:r0 + STRIP, :]
                x = x - jnp.concatenate([d, d], axis=1)
                dec[r0:r0 + STRIP, :] = d + rs[r0:r0 + STRIP, :]
            a = jnp.exp2(x)
            if diag:
                a = jnp.where(before[w] < r0 % qrows, a, 0.0)
            a_s[r0:r0 + STRIP, :w] = a.astype(BF16)
            if w < kw:
                a_s[r0:r0 + STRIP, w:kw] = jnp.zeros((STRIP, kw - w), BF16)

    def values(u, us):
        _, _, _, v_ref, k0, diag, _, acc = unit(u, us)[:8]
        kw = keys(diag)
        if past_transposed and not diag:
            pv = _dot_nt(a_scr[us % 2, :, :kw], v_ref[:, pl.ds(k0, kw)].astype(BF16))
        else:
            pv = jnp.dot(a_scr[us % 2, :, :kw], v_ref[pl.ds(k0, kw), :].astype(BF16),
                         preferred_element_type=F32)
        if diag:
            acc[...] = pv
        else:
            acc[...] += pv
        if len(unit(u, us)) > 8:
            unit(u, us)[8]()

    return scores, terms, weights, values


def _sb_live(dec):
    return (jnp.min(dec[...]) < SB_DEAD).astype(jnp.int32)


def _sb_deeper(stages, j0, dec):
    def body(c):
        for stage in stages:
            stage(c[0], 0)
        return c[0] - 1, _sb_live(dec)

    lax.while_loop(lambda c: (c[0] >= 0) & (c[1] > 0), body, (jnp.asarray(j0, jnp.int32), jnp.int32(1)))


def _sb_scratch(n_qblocks, qrows=SB_BLK):
    rows = 2 * qrows
    return [pltpu.VMEM((4, rows, SB_BLK), F32), pltpu.VMEM((2, rows, SB_BLK), BF16),
            pltpu.VMEM((2, rows, SB_BLK), BF16), pltpu.VMEM((2, rows, LANES), F32),
            pltpu.VMEM((n_qblocks, rows, LANES), F32), pltpu.VMEM((n_qblocks, rows, LANES), F32),
            pltpu.SMEM((n_qblocks,), jnp.int32)]


def _sb_output(acc, sg):
    qrows = sg.shape[0]
    low = _low_half((qrows, LANES))
    return (jnp.where(low, acc[:qrows, :], acc[qrows:, :]) * sg.astype(F32)).astype(BF16)


def _attn_b_kernel(q_ref, k_ref, v_ref, sg_ref, o_ref, z_scr, h_scr, a_scr, rs_scr, dec_all, acc_all, live):
    n_q = q_ref.shape[0] // SB_BLK
    scr = (z_scr, h_scr, a_scr, rs_scr)
    blk = lambda i: i * SB_BLK if isinstance(i, int) else pl.multiple_of(i * SB_BLK, SB_BLK)

    def emit(i):
        rows = pl.ds(blk(i), SB_BLK)
        o_ref[rows, :] = _sb_output(acc_all[i], sg_ref[rows, :])

    def unit(u, us):
        i = (u + 1) // 2
        diag = u % 2 == 1 or u == 0 if isinstance(u, int) else us % 2 == 1
        base = (q_ref, blk(i), k_ref, v_ref, blk(i if diag else i - 1), diag, dec_all.at[i], acc_all.at[i])
        if diag and not (isinstance(u, int) and u == 0):
            return base

        def done():
            emit(i)
            live[i] = _sb_live(dec_all.at[i])

        return base + (done,)

    _pipeline(2 * n_q - 1, 1, 4, _sb_stages(scr, unit))

    def finish(i, c):
        @pl.when(live[i] > 0)
        def _():
            unit_at = lambda j, us: (q_ref, blk(i), k_ref, v_ref, blk(j), False, dec_all.at[i], acc_all.at[i])
            _sb_deeper(_sb_stages(scr, unit_at), i - 2, dec_all.at[i])
            emit(i)

        return c

    lax.fori_loop(2, n_q, finish, 0)


def _attn_b_cached_kernel(q_ref, kd_ref, vd_ref, kp_ref, vp_ref, sg_ref, o_ref,
                          z_scr, h_scr, a_scr, rs_scr, dec_all, acc_all, live, kd_pad, vd_pad):
    n_past = kp_ref.shape[1] // SB_BLK
    scr = (z_scr, h_scr, a_scr, rs_scr)
    state = (dec_all.at[0], acc_all.at[0])
    qrows = q_ref.shape[0]
    shape = dict(past_transposed=True, qrows=qrows, diag_w=LANES)
    for pad, src in ((kd_pad, kd_ref), (vd_pad, vd_ref)):
        pad[...] = jnp.zeros(pad.shape, BF16)
        pad[:qrows, :] = src[...]

    def emit():
        o_ref[...] = _sb_output(acc_all[0], sg_ref[...])

    def done():
        emit()
        live[0] = _sb_live(dec_all.at[0])

    def unit(u, us):
        if u == 0:
            return (q_ref, 0, kd_pad, vd_pad, 0, True) + state
        return (q_ref, 0, kp_ref, vp_ref, (n_past - 1) * SB_BLK, False) + state + (done,)

    _pipeline(2, 2, 4, _sb_stages(scr, unit, **shape))

    @pl.when(live[0] > 0)
    def _():
        unit_at = lambda j, us: (q_ref, 0, kp_ref, vp_ref, pl.multiple_of(j * SB_BLK, SB_BLK), False) + state
        _sb_deeper(_sb_stages(scr, unit_at, **shape), n_past - 2, dec_all.at[0])
        emit()


def _attn_b(q, k, v, sg):
    B, S, _ = q.shape
    spec = pl.BlockSpec((None, S, LANES), lambda p, b: (b, 0, p))
    return pl.pallas_call(
        _attn_b_kernel,
        grid=(W_B // LANES, B),
        in_specs=[spec] * 4,
        out_specs=spec,
        out_shape=jax.ShapeDtypeStruct(q.shape, BF16),
        scratch_shapes=_sb_scratch(S // SB_BLK),
        compiler_params=_params(2),
        name="attn_b",
    )(q, k, v, sg)


def _attn_b_cached(q, kd, vd, kp, vp, sg):
    B, S, _ = q.shape
    P = kp.shape[2]
    assert S % STRIP == 0 and S <= LANES and P % SB_BLK == 0
    spec = pl.BlockSpec((None, S, LANES), lambda p, b: (b, 0, p))
    pspec = pl.BlockSpec((None, LANES, P), lambda p, b: (b, p, 0))
    return pl.pallas_call(
        _attn_b_cached_kernel,
        grid=(W_B // LANES, B),
        in_specs=[spec, spec, spec, pspec, pspec, spec],
        out_specs=spec,
        out_shape=jax.ShapeDtypeStruct(q.shape, BF16),
        scratch_shapes=_sb_scratch(1, S) + [pltpu.VMEM((LANES, LANES), BF16)] * 2,
        compiler_params=_params(2),
        name="attn_b_cached",
    )(q, kd, vd, kp, vp, sg)


def _rope_table_kernel(inv_ref, cos_ref, sin_ref, *, pos0, period):
    shape = cos_ref.shape
    row = lax.broadcasted_iota(jnp.int32, shape, 0)
    lane = lax.broadcasted_iota(jnp.int32, shape, 1)
    ang = (pos0 + row % period).astype(F32) * inv_ref[...]
    cos_ref[...] = jnp.cos(ang)
    s = jnp.sin(ang)
    sin_ref[...] = jnp.where(lane % HEAD_DIM < HEAD_DIM // 2, -s, s)


def _rope_tables(n_rows, pos0, period):
    half = HEAD_DIM // 2
    inv = ROPE_THETA ** (-jnp.arange(half, dtype=F32) * (2.0 / HEAD_DIM))
    inv = jnp.tile(inv, LANES // half)[None, :]
    shape = jax.ShapeDtypeStruct((n_rows, LANES), F32)
    return pl.pallas_call(
        functools.partial(_rope_table_kernel, pos0=pos0, period=period),
        out_shape=[shape, shape],
        name="rope_table",
    )(inv)


def _proj_c_kernel(*refs, n_parts):
    if n_parts:
        w_out, g_post, x_ref, g_ref, w_ref, cos_ref, sin_ref, y_ref = refs[n_parts:n_parts + 8]
        x = _merge_body(refs[:n_parts], w_out, g_post, x_ref, y_ref)
        _proj_c_body(x, g_ref, w_ref, cos_ref, sin_ref, *refs[n_parts + 8:])
    else:
        _proj_c_body(refs[0][...], *refs[1:])


def _proj_c_body(x, g_ref, w_ref, cos_ref, sin_ref, q_o, k_o, v_o, sg_o, kf_o, vf_o):
    xn = (x * _rms_scale(x) * g_ref[...]).astype(BF16)
    tm = x.shape[0]
    keep = kf_o.shape[0]

    def keep_tail(o_ref, u):
        o_ref[...] = u[tm - keep:, :]

    cos = cos_ref[...]
    sin = sin_ref[...]
    first = lax.broadcasted_iota(jnp.int32, cos.shape, 1) % HEAD_DIM < HEAD_DIM // 2

    def rope(u):
        partner = jnp.where(first, pltpu.roll(u, LANES - HEAD_DIM // 2, 1), pltpu.roll(u, HEAD_DIM // 2, 1))
        return u * cos + partner * sin

    blk = 4 * LANES
    for c0 in range(0, W_C, blk):
        u = jnp.dot(xn, w_ref[:, c0:c0 + blk], preferred_element_type=F32)
        for j in range(0, blk, LANES):
            q_o[:, c0 + j:c0 + j + LANES] = (rope(u[:, j:j + LANES]) * (SCALE * LOG2E)).astype(BF16)
    u = jnp.dot(xn, w_ref[:, W_C:W_C + KVW_C], preferred_element_type=F32)
    r = jnp.concatenate([rope(u[:, j:j + LANES]) for j in range(0, KVW_C, LANES)], axis=1)
    k_o[...] = r.astype(BF16)
    keep_tail(kf_o, r)
    u = jnp.dot(xn, w_ref[:, W_C + KVW_C:W_C + 2 * KVW_C], preferred_element_type=F32)
    v_o[...] = u.astype(BF16)
    keep_tail(vf_o, u)
    g0 = W_C + 2 * KVW_C
    for c0 in range(0, W_C, blk):
        u = jnp.dot(xn, w_ref[:, g0 + c0:g0 + c0 + blk], preferred_element_type=F32)
        sg_o[:, c0:c0 + blk] = _silu(u).astype(BF16)


def _proj_c(x2d, g_pre, w_bf16, cos, sin, tm, tiles_per_seq, keep, merge=None):
    parts, w_out, g_post, x_res = merge if merge is not None else ((), None, None, None)
    T = (x_res if merge is not None else x2d).shape[0]
    n_tiles = T // tm
    nt = cos.shape[0] // tm
    row = lambda w: pl.BlockSpec((tm, w), lambda i: (i, 0))
    const = lambda a: pl.BlockSpec(a.shape, lambda i: (0, 0), pipeline_mode=pl.Buffered(1))
    tab = pl.BlockSpec((tm, LANES), lambda i: (i % nt, 0))
    tail = pl.BlockSpec((keep, KVW_C), lambda i: (i // tiles_per_seq, 0))
    sd = jax.ShapeDtypeStruct
    tail_shape = sd((n_tiles // tiles_per_seq * keep, KVW_C), F32)
    if merge is not None:
        operands = [*parts, w_out, g_post, x_res]
        in_specs = [row(p.shape[1]) for p in parts] + [const(w_out), const(g_post), row(D_MODEL)]
        out_specs, out_shape = [row(D_MODEL)], [sd((T, D_MODEL), F32)]
    else:
        operands, in_specs, out_specs, out_shape = [x2d], [row(D_MODEL)], [], []
    return pl.pallas_call(
        functools.partial(_proj_c_kernel, n_parts=len(parts)),
        grid=(n_tiles,),
        in_specs=in_specs + [const(g_pre), const(w_bf16), tab, tab],
        out_specs=out_specs + [row(W_C), row(KVW_C), row(KVW_C), row(W_C), tail, tail],
        out_shape=out_shape + [sd((T, W_C), BF16), sd((T, KVW_C), BF16), sd((T, KVW_C), BF16),
                               sd((T, W_C), BF16), tail_shape, tail_shape],
        compiler_params=_params(1, "arbitrary"),
        name="merge_proj_c" if merge is not None else "proj_c",
    )(*operands, g_pre, w_bf16, cos, sin)


def _attn_c_stages(sink_ref, q_ref, k_ref, v_ref, sg_ref, o_ref, s_scr, p_scr, e_scr, geom, n_valid):
    low = _low_half((QROWS, LANES))
    blocks = [(hh, g) for hh in range(2) for g in range(G_C)]

    def qk(m, slot):
        q0, k0, width, mcol0 = geom(m)
        i = lax.broadcasted_iota(jnp.int32, (QROWS, width), 0) // CHUNK
        j = lax.broadcasted_iota(jnp.int32, (QROWS, width), 1)
        jc = (j + mcol0) // CHUNK
        valid = (jc <= i + WINDOW_C // CHUNK) & (jc >= i)
        if n_valid is not None:
            valid = valid & (j < n_valid)
        mask = jnp.where(valid, 0.0, NEG_INF)
        rows = pl.ds(q0, QROWS)
        q2 = jnp.concatenate([_head_of_pair(q_ref[rows, g * LANES:(g + 1) * LANES], low, hh)
                              for hh, g in blocks], axis=0)
        z = _dot_nt(q2, k_ref[pl.ds(k0, width), :])
        for bi in range(len(blocks)):
            s_scr[slot, bi * QROWS:(bi + 1) * QROWS, :width] = z[bi * QROWS:(bi + 1) * QROWS] + mask

    def sm(m, slot):
        width = geom(m)[2]
        pair = pl.program_id(0)
        for bi, (hh, g) in enumerate(blocks):
            sink = sink_ref[(2 * pair + hh) * G_C + g] * LOG2E
            for r in range(bi * QROWS, (bi + 1) * QROWS, STRIP):
                x = s_scr[slot, r:r + STRIP, :width]
                mx = jnp.maximum(jnp.max(x, axis=-1, keepdims=True), sink)
                p_scr[slot, r:r + STRIP, :width] = jnp.exp2(x - mx).astype(BF16)
                e_scr[slot, r:r + STRIP, :] = jnp.broadcast_to(jnp.exp2(sink - mx), (STRIP, LANES))

    def pv(m, slot):
        q0, k0, width, _ = geom(m)
        rows = pl.ds(q0, QROWS)
        vw = v_ref[pl.ds(k0, width), :]
        o2 = jnp.dot(p_scr[slot, :, :width], jnp.concatenate([vw, jnp.ones_like(vw)], axis=1),
                     preferred_element_type=F32)
        for g in range(G_C):
            t0, b0 = g * QROWS, (G_C + g) * QROWS
            top = o2[t0:t0 + QROWS, :LANES] / (o2[t0:t0 + QROWS, LANES:] + e_scr[slot, t0:t0 + QROWS, :])
            bot = o2[b0:b0 + QROWS, :LANES] / (o2[b0:b0 + QROWS, LANES:] + e_scr[slot, b0:b0 + QROWS, :])
            cols = slice(g * LANES, (g + 1) * LANES)
            o_ref[rows, cols] = (jnp.where(low, top, bot) * sg_ref[rows, cols].astype(F32)).astype(BF16)

    return qk, sm, pv


def _attn_c_kernel(sink_ref, q_ref, k_ref, v_ref, sg_ref, o_ref, s_scr, p_scr, e_scr, *, n_steps):
    def geom(m):
        if isinstance(m, int) and m == 0:
            return 0, 0, QROWS, C_WIN - QROWS
        if isinstance(m, int):
            return m * QROWS, (m - 1) * QROWS, C_WIN, 0
        return pl.multiple_of(m * QROWS, QROWS), pl.multiple_of((m - 1) * QROWS, QROWS), C_WIN, 0

    _pipeline(n_steps, 1, 2, _attn_c_stages(sink_ref, q_ref, k_ref, v_ref, sg_ref, o_ref,
                                            s_scr, p_scr, e_scr, geom, None))


def _attn_c_window_kernel(sink_ref, q_ref, kt_ref, vt_ref, sg_ref, o_ref, *, n_valid):
    S = q_ref.shape[0]
    low = _low_half((S, LANES))
    pair = pl.program_id(0)
    j = lax.broadcasted_iota(jnp.int32, (S, C_WIN), 1)
    mask = jnp.where((j // CHUNK <= WINDOW_C // CHUNK) & (j < n_valid), 0.0, NEG_INF)
    blocks = [(hh, g) for hh in range(2) for g in range(G_C)]
    q2 = jnp.concatenate([_head_of_pair(q_ref[:, g * LANES:(g + 1) * LANES], low, hh) for hh, g in blocks], axis=0)
    z = jnp.dot(q2, kt_ref[...], preferred_element_type=F32)
    probs, sink_terms = [], []
    for bi, (hh, g) in enumerate(blocks):
        x = z[bi * S:(bi + 1) * S] + mask
        sink = sink_ref[(2 * pair + hh) * G_C + g] * LOG2E
        mx = jnp.maximum(jnp.max(x, axis=-1, keepdims=True), sink)
        probs.append(jnp.exp2(x - mx).astype(BF16))
        sink_terms.append(jnp.exp2(sink - mx))
    vt = vt_ref[...]
    o2 = _dot_nt(jnp.concatenate(probs, axis=0), jnp.concatenate([vt, jnp.ones_like(vt)], axis=0))
    for g in range(G_C):
        halves = []
        for bi in (g, G_C + g):
            blk = o2[bi * S:(bi + 1) * S]
            halves.append(blk[:, :LANES] / (blk[:, LANES:] + sink_terms[bi]))
        cols = slice(g * LANES, (g + 1) * LANES)
        o_ref[:, cols] = (jnp.where(low, *halves) * sg_ref[:, cols].astype(F32)).astype(BF16)


def _attn_c(q, k, v, sg, sinks, n_valid=None):
    B, Sq, _ = q.shape
    qw = W_C // (KVW_C // LANES)
    qspec = pl.BlockSpec((None, Sq, qw), lambda p, b: (b, 0, p))
    sspec = pl.BlockSpec(memory_space=pltpu.SMEM)
    stacked = 2 * G_C * QROWS
    if n_valid is None:
        kern = functools.partial(_attn_c_kernel, n_steps=Sq // QROWS)
        kspec = pl.BlockSpec((None, Sq, LANES), lambda p, b: (b, 0, p))
        scratch = [pltpu.VMEM((2, stacked, C_WIN), F32), pltpu.VMEM((2, stacked, C_WIN), BF16),
                   pltpu.VMEM((2, stacked, LANES), F32)]
    else:
        assert Sq <= CHUNK and k.shape[2] == C_WIN
        kern = functools.partial(_attn_c_window_kernel, n_valid=n_valid)
        kspec = pl.BlockSpec((None, LANES, C_WIN), lambda p, b: (b, p, 0))
        scratch = []
    return pl.pallas_call(
        kern,
        grid=(KVW_C // LANES, B),
        in_specs=[sspec, qspec, kspec, kspec, qspec],
        out_specs=qspec,
        out_shape=jax.ShapeDtypeStruct(q.shape, BF16),
        scratch_shapes=scratch,
        compiler_params=_params(2),
        name="attn_c" if n_valid is None else "attn_c_window",
    )(sinks, q, k, v, sg)


def _c_head_order():
    heads = [(2 * P + hh) * G_C + g for P in range(KV_C // 2) for g in range(G_C) for hh in range(2)]
    return jnp.asarray([h * HEAD_DIM + d for h in heads for d in range(HEAD_DIM)], jnp.int32)


def _heads(x2d, B, h):
    return x2d.reshape(B, x2d.shape[0] // B, h, HEAD_DIM)


def _flush(act):
    if not isinstance(act, tuple):
        return act
    merge, B, S, tm = act
    tm = MERGE_ROWS if (B * S) % MERGE_ROWS == 0 else tm
    return _merge(*merge, tm).reshape(B, S, D_MODEL)


def _ab_layer(x, cache, g_pre, w_in, w_out, g_post, table):
    x = _flush(x)
    B, S, _ = x.shape
    T = B * S
    x2d = x.reshape(T, D_MODEL)
    r3 = lambda t: t.reshape(B, S, -1)

    def cached(t, rows):
        g, w, c = t.shape
        t = t.reshape(g, w // HEAD_DIM, HEAD_DIM, c // rows, rows)
        return jnp.transpose(t, (0, 3, 4, 1, 2)).reshape(B, rows, w // HEAD_DIM, HEAD_DIM)

    if cache is None:
        tm = min(S, 512)
        la = min(LEFT_CHUNKS_A * CHUNK, S)
        assert S % tm == 0 and la <= tm
        proj = _proj_ab(x2d, g_pre[None, :], w_in.astype(BF16), tm, S // tm, la)
        qa, ka, va, sga, qb, kb, vb, sgb = proj[:8]
        oa = _attn_a(r3(qa), r3(ka), r3(va), r3(sga), _bias_table(table, A_WIN), window=False)
        ob = _attn_b(r3(qb), r3(kb), r3(vb), r3(sgb))
    else:
        ca_k, ca_v, cb_k, cb_v = cache
        la = ca_k.shape[1]
        assert la == LEFT_CHUNKS_A * CHUNK and S <= CHUNK and cb_k.shape[1] % SB_BLK == 0
        keys_minor = lambda c: jnp.transpose(c, (0, 2, 3, 1)).reshape(B, c.shape[2] * HEAD_DIM, c.shape[1])
        tm = T
        proj = _proj_ab(x2d, g_pre[None, :], w_in.astype(BF16), tm, 1, tm)
        qa, ka, va, sga, qb, kb, vb, sgb = proj[:8]

        def window(c, new):
            parts = [keys_minor(c).astype(BF16), jnp.swapaxes(r3(new), 1, 2)]
            return jnp.pad(jnp.concatenate(parts, axis=2), ((0, 0), (0, 0), (0, A_WIN - la - S)))

        oa = _attn_a(r3(qa), window(ca_k, ka), window(ca_v, va), r3(sga), _bias_table(table, la + S), window=True)
        ob = _attn_b_cached(r3(qb), r3(kb), r3(vb), keys_minor(cb_k), keys_minor(cb_v), r3(sgb))
    rows = (cached(proj[8], min(la, S)), cached(proj[9], min(la, S)), cached(proj[10], S), cached(proj[11], S))
    merge = ([oa.reshape(T, W_A), ob.reshape(T, W_B)], w_out.astype(BF16), g_post[None, :], x2d)
    return (merge, B, S, tm), rows


def _c_layer(x, cache, past, g_pre, w_in, sinks, w_out, g_post):
    if isinstance(x, tuple):
        merge, B, S, _ = x
        x2d = None
    else:
        merge, (B, S, _) = None, x.shape
        x2d = x.reshape(B * S, D_MODEL)
    T = B * S
    order = _c_head_order()
    g0 = W_C + 2 * KVW_C
    w_perm = jnp.concatenate([w_in[:, :W_C][:, order], w_in[:, W_C:g0], w_in[:, g0:][:, order]], axis=1)
    r3 = lambda t: t.reshape(B, S, -1)
    if cache is None:
        tm = min(S, 512)
        lc = min(WINDOW_C, S)
        assert S % tm == 0 and lc <= tm
        cos, sin = _rope_tables(S, 0, S)
        proj = _proj_c(x2d, g_pre[None, :], w_perm.astype(BF16), cos, sin, tm, S // tm, lc, merge)
        x2d, (q, k, v, sg, kf, vf) = (proj[0], proj[1:]) if merge is not None else (x2d, proj)
        o = _attn_c(r3(q), r3(k), r3(v), r3(sg), sinks)
    else:
        tm = T
        cos, sin = _rope_tables(T, past, S)
        proj = _proj_c(x2d, g_pre[None, :], w_perm.astype(BF16), cos, sin, tm, 1, tm, merge)
        x2d, (q, k, v, sg, kf, vf) = (proj[0], proj[1:]) if merge is not None else (x2d, proj)
        cc_k, cc_v = cache
        lc = cc_k.shape[1]
        assert lc == WINDOW_C and S <= CHUNK

        def window(c, new):
            old = jnp.transpose(c, (0, 2, 3, 1)).reshape(B, KVW_C, lc).astype(BF16)
            parts = [old, jnp.swapaxes(r3(new), 1, 2)]
            return jnp.pad(jnp.concatenate(parts, axis=2), ((0, 0), (0, 0), (0, C_WIN - lc - S)))

        o = _attn_c(r3(q), window(cc_k, k), window(cc_v, v), r3(sg), sinks, n_valid=lc + S)
    rows = (_heads(kf, B, KV_C), _heads(vf, B, KV_C))
    merge = ([o.reshape(T, W_C)], w_out[order, :].astype(BF16), g_post[None, :], x2d)
    return (merge, B, S, tm), rows


def kernel(x_prompt, x_sample, cache_a_k, cache_a_v, cache_b_k, cache_b_v, cache_c_k, cache_c_v,
           ab_norm_pre, ab_w_in, ab_w_out, ab_norm_post, a_rel_bias,
           c_norm_pre, c_w_in, c_sinks, c_w_out, c_norm_post):
    past = cache_b_k.shape[2]
    depth = ab_w_in.shape[0] + c_w_in.shape[0]
    yp, ys = x_prompt, x_sample
    ab_p, ab_s, c_p, c_s = [], [], [], []
    for layer in range(depth):
        i = layer // 2
        if layer % 2 == 0:
            w = (ab_norm_pre[i], ab_w_in[i], ab_w_out[i], ab_norm_post[i], a_rel_bias[i])
            yp, rows = _ab_layer(yp, None, *w)
            ab_p.append(rows)
            ys, rows = _ab_layer(ys, (cache_a_k[i], cache_a_v[i], cache_b_k[i], cache_b_v[i]), *w)
            ab_s.append(rows)
        else:
            w = (c_norm_pre[i], c_w_in[i], c_sinks[i], c_w_out[i], c_norm_post[i])
            yp, rows = _c_layer(yp, None, past, *w)
            c_p.append(rows)
            ys, rows = _c_layer(ys, (cache_c_k[i], cache_c_v[i]), past, *w)
            c_s.append(rows)
    st = lambda rows, j: jnp.stack([r[j] for r in rows])
    return (_flush(yp), _flush(ys),
            st(ab_p, 0), st(ab_p, 1), st(ab_p, 2), st(ab_p, 3), st(c_p, 0), st(c_p, 1),
            st(ab_s, 0), st(ab_s, 1), st(ab_s, 2), st(ab_s, 3), st(c_s, 0), st(c_s, 1))
```

```python
import functools

import jax
import jax.numpy as jnp
from jax import lax
from jax.experimental import pallas as pl
from jax.experimental.pallas import tpu as pltpu

D_MODEL = 1024
CHUNK = 64
HEAD_DIM = 64
H_A = 8
H_B = 8
LEFT_CHUNKS_A = 8
REL_CLIP_A = 128
N_REL_A = 2 * REL_CLIP_A + 1
W_A = H_A * HEAD_DIM
W_B = H_B * HEAD_DIM
H_C = 16
KV_C = 4
G_C = H_C // KV_C
WINDOW_C = 128
W_C = H_C * HEAD_DIM
KVW_C = KV_C * HEAD_DIM
ROPE_THETA = 10000.0
RMS_EPS = 1e-6
NEG_INF = -1e30
SCALE = HEAD_DIM ** -0.5

LANES = 128
QROWS = 128
A_WIN = (LEFT_CHUNKS_A + 2) * CHUNK
A_TAB = 384
A_ROLLW = A_WIN + QROWS
C_WIN = WINDOW_C + QROWS
MERGE_ROWS = 1024
STRIP = 32
LOG2E = 1.4426950408889634
SB_BLK = 256
SB_DEAD = 104.0 * LOG2E
VMEM_LIMIT = 56 * 1024 * 1024
PROJ_ROWS = 1024

F32 = jnp.float32
BF16 = jnp.bfloat16


def _params(n_axes, semantics="parallel"):
    return pltpu.CompilerParams(dimension_semantics=(semantics,) * n_axes,
                                vmem_limit_bytes=VMEM_LIMIT)


def _rms_scale(x):
    return lax.rsqrt(jnp.mean(x * x, axis=-1, keepdims=True) + RMS_EPS)


def _silu(u):
    return u / (1.0 + jnp.exp(-u))


def _low_half(shape):
    return lax.broadcasted_iota(jnp.int32, shape, 1) < HEAD_DIM


def _head_of_pair(x_bf16, low, h):
    xf = x_bf16.astype(F32)
    return jnp.where(low if h == 0 else jnp.logical_not(low), xf, 0.0).astype(BF16)


def _dot_nt(a, b):
    return lax.dot_general(a, b, (((1,), (1,)), ((), ())), preferred_element_type=F32)


def _proj_ab_kernel(x_ref, g_ref, w_ref, qa, ka, va, sga, qb, kb, vb, sgb, kaf, vaf, kbf, vbf,
                    *, tiles_per_seq, keep_a):
    tm = x_ref.shape[0]

    def body(with_tails):
        x = x_ref[...]
        xn = (x * _rms_scale(x) * g_ref[...]).astype(BF16)

        def col(j):
            return jnp.dot(xn, w_ref[:, j * W_A:(j + 1) * W_A], preferred_element_type=F32)

        sga[...] = _silu(col(3)).astype(BF16)
        sgb[...] = _silu(col(7)).astype(BF16)
        u = col(5); kb[...] = u.astype(BF16); kbf[...] = u.T
        u = col(6); vb[...] = u.astype(BF16); vbf[...] = u.T
        u = col(1); ka[...] = u.astype(BF16)
        if with_tails:
            kaf[...] = u[tm - keep_a:, :].T
        u = col(2); va[...] = u.astype(BF16)
        if with_tails:
            vaf[...] = u[tm - keep_a:, :].T
        qa[...] = (col(0) * (SCALE * LOG2E)).astype(BF16)
        qb[...] = (col(4) * (SCALE * LOG2E)).astype(BF16)

    if tiles_per_seq == 1:
        body(True)
    else:
        last = pl.program_id(0) % tiles_per_seq == tiles_per_seq - 1
        pl.when(last)(functools.partial(body, True))
        pl.when(jnp.logical_not(last))(functools.partial(body, False))


def _proj_ab(x2d, g_pre, w_bf16, tm, tiles_per_seq, keep_a):
    T = x2d.shape[0]
    n_tiles = T // tm
    row = lambda w: pl.BlockSpec((tm, w), lambda i: (i, 0))
    const = lambda a: pl.BlockSpec(a.shape, lambda i: (0, 0), pipeline_mode=pl.Buffered(1))
    bshape = jax.ShapeDtypeStruct((T, W_A), BF16)
    groups = n_tiles // tiles_per_seq
    a_spec = pl.BlockSpec((None, W_A, keep_a), lambda i: (i // tiles_per_seq, 0, 0))
    a_shape = jax.ShapeDtypeStruct((groups, W_A, keep_a), F32)
    b_spec = pl.BlockSpec((None, W_B, tm), lambda i: (i // tiles_per_seq, 0, i % tiles_per_seq))
    b_shape = jax.ShapeDtypeStruct((groups, W_B, tiles_per_seq * tm), F32)
    return pl.pallas_call(
        functools.partial(_proj_ab_kernel, tiles_per_seq=tiles_per_seq, keep_a=keep_a),
        grid=(n_tiles,),
        in_specs=[row(D_MODEL), const(g_pre), const(w_bf16)],
        out_specs=[row(W_A)] * 8 + [a_spec, a_spec, b_spec, b_spec],
        out_shape=[bshape] * 8 + [a_shape, a_shape, b_shape, b_shape],
        compiler_params=_params(1, "arbitrary"),
        name="proj_ab",
    )(x2d, g_pre, w_bf16)


def _merge_body(h_refs, w_ref, g_ref, x_ref, y_ref):
    acc = None
    off = 0
    for h in h_refs:
        k = h.shape[1]
        d = jnp.dot(h[...], w_ref[off:off + k, :], preferred_element_type=F32)
        acc = d if acc is None else acc + d
        off += k
    y = x_ref[...] + acc * _rms_scale(acc) * g_ref[...]
    y_ref[...] = y
    return y


def _merge_kernel(*refs, n_parts):
    _merge_body(refs[:n_parts], *refs[n_parts:])


def _merge(parts, w_bf16, g_post, x2d, tm):
    T = x2d.shape[0]
    row = lambda w: pl.BlockSpec((tm, w), lambda i: (i, 0))
    full = lambda a: pl.BlockSpec(a.shape, lambda i: (0, 0))
    return pl.pallas_call(
        functools.partial(_merge_kernel, n_parts=len(parts)),
        grid=(T // tm,),
        in_specs=[row(p.shape[1]) for p in parts] + [full(w_bf16), full(g_post), row(D_MODEL)],
        out_specs=row(D_MODEL),
        out_shape=jax.ShapeDtypeStruct((T, D_MODEL), F32),
        compiler_params=_params(1),
        name="merge",
    )(*parts, w_bf16, g_post, x2d)


def _bias_kernel(tab_ref, out_ref, *, n_valid):
    m = lax.broadcasted_iota(jnp.int32, (A_TAB, A_ROLLW), 1)
    t = lax.broadcasted_iota(jnp.int32, (A_TAB, A_ROLLW), 0)
    idx = jnp.where(m <= A_WIN, jnp.clip(A_WIN - QROWS - m, -REL_CLIP_A, REL_CLIP_A) + REL_CLIP_A,
                    2 * REL_CLIP_A)
    onehot = jnp.where(t == idx, 1.0, 0.0).astype(BF16)
    tab = tab_ref[...]
    hi = tab.astype(BF16)
    r1 = tab - hi.astype(F32)
    mid = r1.astype(BF16)
    lo = (r1 - mid.astype(F32)).astype(BF16)
    src = (jnp.dot(hi, onehot, preferred_element_type=F32)
           + jnp.dot(mid, onehot, preferred_element_type=F32)
           + jnp.dot(lo, onehot, preferred_element_type=F32))
    i = lax.broadcasted_iota(jnp.int32, (QROWS, A_WIN), 0)
    j = lax.broadcasted_iota(jnp.int32, (QROWS, A_WIN), 1)
    qc = LEFT_CHUNKS_A + i // CHUNK
    kc = j // CHUNK
    valid = (kc <= qc) & (kc >= qc - LEFT_CHUNKS_A) & (j < n_valid)
    for h in range(H_A):
        row = jnp.broadcast_to(src[h:h + 1, :], (QROWS, A_ROLLW))
        toep = pltpu.roll(row, 0, 1, stride=1, stride_axis=0)
        out_ref[h * QROWS:(h + 1) * QROWS, :] = jnp.where(valid, toep[:, :A_WIN] * LOG2E, NEG_INF)


def _bias_table(table, n_valid):
    tab = jnp.pad(table, ((0, 0), (0, A_TAB - N_REL_A)))
    return pl.pallas_call(
        functools.partial(_bias_kernel, n_valid=n_valid),
        out_shape=jax.ShapeDtypeStruct((H_A * QROWS, A_WIN), F32),
        name="a_bias",
    )(tab)


def _pipeline(n_units, n_static, period, stages):
    depth = len(stages)

    def tick(t, tmod):
        for lag in reversed(range(depth)):
            u = t - lag
            if not isinstance(u, int) or 0 <= u < n_units:
                stages[lag](u, (tmod - lag) % period)

    first = n_static + depth - 1
    for t in range(min(first, n_units + depth - 1)):
        tick(t, t % period)
    groups = max(n_units - first, 0) // period

    def body(g, c):
        t = first + period * g
        for d in range(period):
            tick(t + d, (first + d) % period)
        return c

    if groups:
        lax.fori_loop(0, groups, body, 0)
    for t in range(max(first, first + period * groups), n_units + depth - 1):
        tick(t, t % period)


def _softmax_strips(s_scr, p_scr, width):
    for r in range(0, s_scr.shape[0], STRIP):
        x = s_scr[r:r + STRIP, :width]
        p_scr[r:r + STRIP, :width] = jnp.exp2(x - jnp.max(x, axis=-1, keepdims=True)).astype(BF16)


def _attn_a_stages(q_ref, k_ref, v_ref, sg_ref, bias_ref, o_ref, s_scr, p_scr, geom):
    low = _low_half((QROWS, LANES))

    def qk(m, slot):
        q0, k0, width, bcol0 = geom(m)
        qm = q_ref[pl.ds(q0, QROWS), :]
        q2 = jnp.concatenate([_head_of_pair(qm, low, 0), _head_of_pair(qm, low, 1)], axis=0)
        s_scr[slot, :, :width] = _dot_nt(q2, k_ref[pl.ds(k0, width), :]) + bias_ref[:, bcol0:bcol0 + width]

    def sm(m, slot):
        _softmax_strips(s_scr.at[slot], p_scr.at[slot], geom(m)[2])

    def pv(m, slot):
        q0, k0, width, _ = geom(m)
        vw = v_ref[pl.ds(k0, width), :]
        o2 = jnp.dot(p_scr[slot, :, :width], jnp.concatenate([vw, jnp.ones_like(vw)], axis=1),
                     preferred_element_type=F32)
        top, bot = o2[:QROWS], o2[QROWS:]
        o = jnp.where(low, top[:, :LANES] / top[:, LANES:], bot[:, :LANES] / bot[:, LANES:])
        o_ref[pl.ds(q0, QROWS), :] = (o * sg_ref[pl.ds(q0, QROWS), :].astype(F32)).astype(BF16)

    return qk, sm, pv


def _attn_a_kernel(q_ref, k_ref, v_ref, sg_ref, bias_ref, o_ref, s_scr, p_scr, *, n_steps):
    lead = A_WIN // QROWS - 1

    def geom(m):
        if isinstance(m, int) and m < lead:
            return m * QROWS, 0, (m + 1) * QROWS, (lead - m) * QROWS
        if isinstance(m, int):
            return m * QROWS, (m - lead) * QROWS, A_WIN, 0
        return pl.multiple_of(m * QROWS, QROWS), pl.multiple_of((m - lead) * QROWS, QROWS), A_WIN, 0

    _pipeline(n_steps, lead, 2, _attn_a_stages(q_ref, k_ref, v_ref, sg_ref, bias_ref, o_ref, s_scr, p_scr, geom))


def _attn_a_window_kernel(q_ref, kt_ref, vt_ref, sg_ref, bias_ref, o_ref, s_scr, p_scr):
    S = q_ref.shape[0]
    low = _low_half((S, LANES))
    qm = q_ref[...]
    q2 = jnp.concatenate([_head_of_pair(qm, low, 0), _head_of_pair(qm, low, 1)], axis=0)
    bias = jnp.concatenate([bias_ref[:S, :], bias_ref[QROWS:QROWS + S, :]], axis=0)
    s_view, p_view = s_scr.at[0, pl.ds(0, 2 * S)], p_scr.at[0, pl.ds(0, 2 * S)]
    s_view[...] = jnp.dot(q2, kt_ref[...], preferred_element_type=F32) + bias
    _softmax_strips(s_view, p_view, A_WIN)
    vt = vt_ref[...]
    o2 = _dot_nt(p_view[...], jnp.concatenate([vt, jnp.ones_like(vt)], axis=0))
    o = jnp.where(low, o2[:S, :LANES] / o2[:S, LANES:], o2[S:, :LANES] / o2[S:, LANES:])
    o_ref[...] = (o * sg_ref[...].astype(F32)).astype(BF16)


def _attn_a(q, k, v, sg, bias, window):
    B, Sq, _ = q.shape
    qspec = pl.BlockSpec((None, Sq, LANES), lambda p, b: (b, 0, p))
    if window:
        assert Sq % STRIP == 0 and Sq <= CHUNK and k.shape[2] == A_WIN
        kspec = pl.BlockSpec((None, LANES, A_WIN), lambda p, b: (b, p, 0))
    else:
        kspec = pl.BlockSpec((None, Sq, LANES), lambda p, b: (b, 0, p))
    bspec = pl.BlockSpec((2 * QROWS, A_WIN), lambda p, b: (p, 0))
    kern = _attn_a_window_kernel if window else functools.partial(_attn_a_kernel, n_steps=Sq // QROWS)
    return pl.pallas_call(
        kern,
        grid=(W_A // LANES, B),
        in_specs=[qspec, kspec, kspec, qspec, bspec],
        out_specs=qspec,
        out_shape=jax.ShapeDtypeStruct(q.shape, BF16),
        scratch_shapes=[pltpu.VMEM((2, 2 * QROWS, A_WIN), F32), pltpu.VMEM((2, 2 * QROWS, A_WIN), BF16)],
        compiler_params=_params(2),
        name="attn_a_window" if window else "attn_a",
    )(q, k, v, sg, bias)


def _sb_stages(scr, unit, past_transposed=False, qrows=SB_BLK, diag_w=SB_BLK):
    z_scr, h_scr, a_scr, rs_scr = scr
    rows = 2 * qrows
    low = _low_half((qrows, LANES))
    before = {w: lax.broadcasted_iota(jnp.int32, (STRIP, w), 1) - lax.broadcasted_iota(jnp.int32, (STRIP, w), 0)
              for w in {LANES, SB_BLK, diag_w}}
    tri = {}
    for kw in {SB_BLK, diag_w}:
        r = lax.broadcasted_iota(jnp.int32, (kw, kw), 0)
        c = lax.broadcasted_iota(jnp.int32, (kw, kw), 1)
        tri[kw] = jnp.where(r > c, 1.0, 0.0).astype(BF16)
    keys = lambda diag: diag_w if diag else SB_BLK

    def scores(u, us):
        q_ref, q0, k_ref, _, k0, diag = unit(u, us)[:6]
        qm = q_ref[pl.ds(q0, qrows), :]
        q2 = jnp.concatenate([_head_of_pair(qm, low, 0), _head_of_pair(qm, low, 1)], axis=0)
        kw = keys(diag)
        if past_transposed and not diag:
            z_scr[us % 4, :, :kw] = jnp.dot(q2, k_ref[:, pl.ds(k0, kw)].astype(BF16), preferred_element_type=F32)
        else:
            z_scr[us % 4, :, :kw] = _dot_nt(q2, k_ref[pl.ds(k0, kw), :].astype(BF16))

    def width(diag, r0):
        return LANES if diag and r0 % qrows + STRIP <= LANES else keys(diag)

    def terms(u, us):
        diag = unit(u, us)[5]
        zs, hs, rs = z_scr.at[us % 4], h_scr.at[us % 2], rs_scr.at[us % 2]
        for r0 in range(0, rows, STRIP):
            w = width(diag, r0)
            z = zs[r0:r0 + STRIP, :w]
            sp = jnp.maximum(z, 0.0) + jnp.log2(1.0 + jnp.exp2(-jnp.abs(z)))
            zs[r0:r0 + STRIP, :w] = z - sp
            if diag:
                sp = jnp.where(before[w] < r0 % qrows, sp, 0.0)
            hs[r0:r0 + STRIP, :w] = sp.astype(BF16)
            if w < keys(diag):
                hs[r0:r0 + STRIP, w:keys(diag)] = jnp.zeros((STRIP, keys(diag) - w), BF16)
            rs[r0:r0 + STRIP, :] = jnp.broadcast_to(jnp.sum(sp, axis=-1, keepdims=True), (STRIP, LANES))

    def weights(u, us):
        diag, dec = unit(u, us)[5:7]
        zs, rs, a_s = z_scr.at[us % 4], rs_scr.at[us % 2], a_scr.at[us % 2]
        kw = keys(diag)
        inner = jnp.dot(h_scr[us % 2, :, :kw], tri[kw], preferred_element_type=F32)
        for r0 in range(0, rows, STRIP):
            w = width(diag, r0)
            x = zs[r0:r0 + STRIP, :w] - inner[r0:r0 + STRIP, :w]
            if diag:
                dec[r0:r0 + STRIP, :] = rs[r0:r0 + STRIP, :]
            else:
                d = dec[r0:r0 + STRIP, :]
                x = x - jnp.concatenate([d, d], axis=1)
                dec[r0:r0 + STRIP, :] = d + rs[r0:r0 + STRIP, :]
            a = jnp.exp2(x)
            if diag:
                a = jnp.where(before[w] < r0 % qrows, a, 0.0)
            a_s[r0:r0 + STRIP, :w] = a.astype(BF16)
            if w < kw:
                a_s[r0:r0 + STRIP, w:kw] = jnp.zeros((STRIP, kw - w), BF16)

    def values(u, us):
        _, _, _, v_ref, k0, diag, _, acc = unit(u, us)[:8]
        kw = keys(diag)
        if past_transposed and not diag:
            pv = _dot_nt(a_scr[us % 2, :, :kw], v_ref[:, pl.ds(k0, kw)].astype(BF16))
        else:
            pv = jnp.dot(a_scr[us % 2, :, :kw], v_ref[pl.ds(k0, kw), :].astype(BF16),
                         preferred_element_type=F32)
        if diag:
            acc[...] = pv
        else:
            acc[...] += pv
        if len(unit(u, us)) > 8:
            unit(u, us)[8]()

    return scores, terms, weights, values


def _sb_live(dec):
    return (jnp.min(dec[...]) < SB_DEAD).astype(jnp.int32)


def _sb_deeper(stages, j0, dec, fetch=None):
    def body(c):
        if fetch is not None:
            fetch(c[0])
        for stage in stages:
            stage(c[0], 0)
        return c[0] - 1, _sb_live(dec)

    lax.while_loop(lambda c: (c[0] >= 0) & (c[1] > 0), body, (jnp.asarray(j0, jnp.int32), jnp.int32(1)))


def _sb_scratch(n_qblocks, qrows=SB_BLK):
    rows = 2 * qrows
    return [pltpu.VMEM((4, rows, SB_BLK), F32), pltpu.VMEM((2, rows, SB_BLK), BF16),
            pltpu.VMEM((2, rows, SB_BLK), BF16), pltpu.VMEM((2, rows, LANES), F32),
            pltpu.VMEM((n_qblocks, rows, LANES), F32), pltpu.VMEM((n_qblocks, rows, LANES), F32),
            pltpu.SMEM((n_qblocks,), jnp.int32)]


def _sb_output(acc, sg):
    qrows = sg.shape[0]
    low = _low_half((qrows, LANES))
    return (jnp.where(low, acc[:qrows, :], acc[qrows:, :]) * sg.astype(F32)).astype(BF16)


def _attn_b_kernel(q_ref, k_ref, v_ref, sg_ref, o_ref, z_scr, h_scr, a_scr, rs_scr, dec_all, acc_all, live):
    n_q = q_ref.shape[0] // SB_BLK
    scr = (z_scr, h_scr, a_scr, rs_scr)
    blk = lambda i: i * SB_BLK if isinstance(i, int) else pl.multiple_of(i * SB_BLK, SB_BLK)

    def emit(i):
        rows = pl.ds(blk(i), SB_BLK)
        o_ref[rows, :] = _sb_output(acc_all[i], sg_ref[rows, :])

    def unit(u, us):
        i = (u + 1) // 2
        diag = u % 2 == 1 or u == 0 if isinstance(u, int) else us % 2 == 1
        base = (q_ref, blk(i), k_ref, v_ref, blk(i if diag else i - 1), diag, dec_all.at[i], acc_all.at[i])
        if diag and not (isinstance(u, int) and u == 0):
            return base

        def done():
            emit(i)
            live[i] = _sb_live(dec_all.at[i])

        return base + (done,)

    _pipeline(2 * n_q - 1, 1, 4, _sb_stages(scr, unit))

    def finish(i, c):
        @pl.when(live[i] > 0)
        def _():
            unit_at = lambda j, us: (q_ref, blk(i), k_ref, v_ref, blk(j), False, dec_all.at[i], acc_all.at[i])
            _sb_deeper(_sb_stages(scr, unit_at), i - 2, dec_all.at[i])
            emit(i)

        return c

    lax.fori_loop(2, n_q, finish, 0)


def _attn_b_cached_kernel(q_ref, kd_ref, vd_ref, kl_ref, vl_ref, kp_hbm, vp_hbm, sg_ref, o_ref,
                          z_scr, h_scr, a_scr, rs_scr, dec_all, acc_all, live, kd_pad, vd_pad, kbuf, vbuf, sem):
    n_past = kp_hbm.shape[2] // SB_BLK
    pair, batch = pl.program_id(0), pl.program_id(1)
    scr = (z_scr, h_scr, a_scr, rs_scr)
    state = (dec_all.at[0], acc_all.at[0])
    qrows = q_ref.shape[0]
    shape = dict(past_transposed=True, qrows=qrows, diag_w=LANES)
    for pad, src in ((kd_pad, kd_ref), (vd_pad, vd_ref)):
        pad[...] = jnp.zeros(pad.shape, BF16)
        pad[:qrows, :] = src[...]

    def emit():
        o_ref[...] = _sb_output(acc_all[0], sg_ref[...])

    def done():
        emit()
        live[0] = _sb_live(dec_all.at[0])

    def unit(u, us):
        if u == 0:
            return (q_ref, 0, kd_pad, vd_pad, 0, True) + state
        return (q_ref, 0, kl_ref, vl_ref, 0, False) + state + (done,)

    _pipeline(2, 2, 4, _sb_stages(scr, unit, **shape))

    @pl.when(live[0] > 0)
    def _():
        def fetch(j):
            rows = pl.ds(pl.multiple_of(pair * LANES, LANES), LANES)
            cols = pl.ds(pl.multiple_of(j * SB_BLK, SB_BLK), SB_BLK)
            copies = [pltpu.make_async_copy(hbm.at[batch, rows, cols], buf, sem.at[n])
                      for n, (hbm, buf) in enumerate(((kp_hbm, kbuf), (vp_hbm, vbuf)))]
            for c in copies:
                c.start()
            for c in copies:
                c.wait()

        unit_at = lambda j, us: (q_ref, 0, kbuf, vbuf, 0, False) + state
        _sb_deeper(_sb_stages(scr, unit_at, **shape), n_past - 2, dec_all.at[0], fetch)
        emit()


def _attn_b(q, k, v, sg):
    B, S, _ = q.shape
    spec = pl.BlockSpec((None, S, LANES), lambda p, b: (b, 0, p))
    return pl.pallas_call(
        _attn_b_kernel,
        grid=(W_B // LANES, B),
        in_specs=[spec] * 4,
        out_specs=spec,
        out_shape=jax.ShapeDtypeStruct(q.shape, BF16),
        scratch_shapes=_sb_scratch(S // SB_BLK),
        compiler_params=_params(2),
        name="attn_b",
    )(q, k, v, sg)


def _attn_b_cached(q, kd, vd, kp, vp, sg):
    B, S, _ = q.shape
    P = kp.shape[2]
    assert S % STRIP == 0 and S <= LANES and P % SB_BLK == 0
    spec = pl.BlockSpec((None, S, LANES), lambda p, b: (b, 0, p))
    last = pl.BlockSpec((None, LANES, SB_BLK), lambda p, b: (b, p, P // SB_BLK - 1))
    hbm = pl.BlockSpec(memory_space=pl.ANY)
    return pl.pallas_call(
        _attn_b_cached_kernel,
        grid=(W_B // LANES, B),
        in_specs=[spec, spec, spec, last, last, hbm, hbm, spec],
        out_specs=spec,
        out_shape=jax.ShapeDtypeStruct(q.shape, BF16),
        scratch_shapes=(_sb_scratch(1, S) + [pltpu.VMEM((LANES, LANES), BF16)] * 2
                        + [pltpu.VMEM((LANES, SB_BLK), F32)] * 2 + [pltpu.SemaphoreType.DMA((2,))]),
        compiler_params=_params(2),
        name="attn_b_cached",
    )(q, kd, vd, kp, vp, kp, vp, sg)


def _rope_table_kernel(inv_ref, cos_ref, sin_ref, *, pos0, period):
    shape = cos_ref.shape
    row = lax.broadcasted_iota(jnp.int32, shape, 0)
    lane = lax.broadcasted_iota(jnp.int32, shape, 1)
    ang = (pos0 + row % period).astype(F32) * inv_ref[...]
    cos_ref[...] = jnp.cos(ang)
    s = jnp.sin(ang)
    sin_ref[...] = jnp.where(lane % HEAD_DIM < HEAD_DIM // 2, -s, s)


def _rope_tables(n_rows, pos0, period):
    half = HEAD_DIM // 2
    inv = ROPE_THETA ** (-jnp.arange(half, dtype=F32) * (2.0 / HEAD_DIM))
    inv = jnp.tile(inv, LANES // half)[None, :]
    shape = jax.ShapeDtypeStruct((n_rows, LANES), F32)
    return pl.pallas_call(
        functools.partial(_rope_table_kernel, pos0=pos0, period=period),
        out_shape=[shape, shape],
        name="rope_table",
    )(inv)


def _proj_c_kernel(*refs, n_parts):
    if n_parts:
        w_out, g_post, x_ref, g_ref, w_ref, cos_ref, sin_ref, y_ref = refs[n_parts:n_parts + 8]
        x = _merge_body(refs[:n_parts], w_out, g_post, x_ref, y_ref)
        _proj_c_body(x, g_ref, w_ref, cos_ref, sin_ref, *refs[n_parts + 8:])
    else:
        _proj_c_body(refs[0][...], *refs[1:])


def _proj_c_body(x, g_ref, w_ref, cos_ref, sin_ref, q_o, k_o, v_o, sg_o, kf_o, vf_o):
    xn = (x * _rms_scale(x) * g_ref[...]).astype(BF16)
    tm = x.shape[0]
    keep = kf_o.shape[0]

    def keep_tail(o_ref, u):
        o_ref[...] = u[tm - keep:, :]

    cos = cos_ref[...]
    sin = sin_ref[...]
    first = lax.broadcasted_iota(jnp.int32, cos.shape, 1) % HEAD_DIM < HEAD_DIM // 2

    def rope(u):
        partner = jnp.where(first, pltpu.roll(u, LANES - HEAD_DIM // 2, 1), pltpu.roll(u, HEAD_DIM // 2, 1))
        return u * cos + partner * sin

    blk = 4 * LANES
    for c0 in range(0, W_C, blk):
        u = jnp.dot(xn, w_ref[:, c0:c0 + blk], preferred_element_type=F32)
        for j in range(0, blk, LANES):
            q_o[:, c0 + j:c0 + j + LANES] = (rope(u[:, j:j + LANES]) * (SCALE * LOG2E)).astype(BF16)
    u = jnp.dot(xn, w_ref[:, W_C:W_C + KVW_C], preferred_element_type=F32)
    r = jnp.concatenate([rope(u[:, j:j + LANES]) for j in range(0, KVW_C, LANES)], axis=1)
    k_o[...] = r.astype(BF16)
    keep_tail(kf_o, r)
    u = jnp.dot(xn, w_ref[:, W_C + KVW_C:W_C + 2 * KVW_C], preferred_element_type=F32)
    v_o[...] = u.astype(BF16)
    keep_tail(vf_o, u)
    g0 = W_C + 2 * KVW_C
    for c0 in range(0, W_C, blk):
        u = jnp.dot(xn, w_ref[:, g0 + c0:g0 + c0 + blk], preferred_element_type=F32)
        sg_o[:, c0:c0 + blk] = _silu(u).astype(BF16)


def _proj_c(x2d, g_pre, w_bf16, cos, sin, tm, tiles_per_seq, keep, merge=None):
    parts, w_out, g_post, x_res = merge if merge is not None else ((), None, None, None)
    T = (x_res if merge is not None else x2d).shape[0]
    n_tiles = T // tm
    nt = cos.shape[0] // tm
    row = lambda w: pl.BlockSpec((tm, w), lambda i: (i, 0))
    const = lambda a: pl.BlockSpec(a.shape, lambda i: (0, 0), pipeline_mode=pl.Buffered(1))
    tab = pl.BlockSpec((tm, LANES), lambda i: (i % nt, 0))
    tail = pl.BlockSpec((keep, KVW_C), lambda i: (i // tiles_per_seq, 0))
    sd = jax.ShapeDtypeStruct
    tail_shape = sd((n_tiles // tiles_per_seq * keep, KVW_C), F32)
    if merge is not None:
        operands = [*parts, w_out, g_post, x_res]
        in_specs = [row(p.shape[1]) for p in parts] + [const(w_out), const(g_post), row(D_MODEL)]
        out_specs, out_shape = [row(D_MODEL)], [sd((T, D_MODEL), F32)]
    else:
        operands, in_specs, out_specs, out_shape = [x2d], [row(D_MODEL)], [], []
    return pl.pallas_call(
        functools.partial(_proj_c_kernel, n_parts=len(parts)),
        grid=(n_tiles,),
        in_specs=in_specs + [const(g_pre), const(w_bf16), tab, tab],
        out_specs=out_specs + [row(W_C), row(KVW_C), row(KVW_C), row(W_C), tail, tail],
        out_shape=out_shape + [sd((T, W_C), BF16), sd((T, KVW_C), BF16), sd((T, KVW_C), BF16),
                               sd((T, W_C), BF16), tail_shape, tail_shape],
        compiler_params=_params(1, "arbitrary"),
        name="merge_proj_c" if merge is not None else "proj_c",
    )(*operands, g_pre, w_bf16, cos, sin)


def _attn_c_stages(sink_ref, q_ref, k_ref, v_ref, sg_ref, o_ref, s_scr, p_scr, e_scr, geom, n_valid):
    low = _low_half((QROWS, LANES))
    blocks = [(hh, g) for hh in range(2) for g in range(G_C)]

    def qk(m, slot):
        q0, k0, width, mcol0 = geom(m)
        i = lax.broadcasted_iota(jnp.int32, (QROWS, width), 0) // CHUNK
        j = lax.broadcasted_iota(jnp.int32, (QROWS, width), 1)
        jc = (j + mcol0) // CHUNK
        valid = (jc <= i + WINDOW_C // CHUNK) & (jc >= i)
        if n_valid is not None:
            valid = valid & (j < n_valid)
        mask = jnp.where(valid, 0.0, NEG_INF)
        rows = pl.ds(q0, QROWS)
        q2 = jnp.concatenate([_head_of_pair(q_ref[rows, g * LANES:(g + 1) * LANES], low, hh)
                              for hh, g in blocks], axis=0)
        z = _dot_nt(q2, k_ref[pl.ds(k0, width), :])
        for bi in range(len(blocks)):
            s_scr[slot, bi * QROWS:(bi + 1) * QROWS, :width] = z[bi * QROWS:(bi + 1) * QROWS] + mask

    def sm(m, slot):
        width = geom(m)[2]
        pair = pl.program_id(0)
        for bi, (hh, g) in enumerate(blocks):
            sink = sink_ref[(2 * pair + hh) * G_C + g] * LOG2E
            for r in range(bi * QROWS, (bi + 1) * QROWS, STRIP):
                x = s_scr[slot, r:r + STRIP, :width]
                mx = jnp.maximum(jnp.max(x, axis=-1, keepdims=True), sink)
                p_scr[slot, r:r + STRIP, :width] = jnp.exp2(x - mx).astype(BF16)
                e_scr[slot, r:r + STRIP, :] = jnp.broadcast_to(jnp.exp2(sink - mx), (STRIP, LANES))

    def pv(m, slot):
        q0, k0, width, _ = geom(m)
        rows = pl.ds(q0, QROWS)
        vw = v_ref[pl.ds(k0, width), :]
        o2 = jnp.dot(p_scr[slot, :, :width], jnp.concatenate([vw, jnp.ones_like(vw)], axis=1),
                     preferred_element_type=F32)
        for g in range(G_C):
            t0, b0 = g * QROWS, (G_C + g) * QROWS
            top = o2[t0:t0 + QROWS, :LANES] / (o2[t0:t0 + QROWS, LANES:] + e_scr[slot, t0:t0 + QROWS, :])
            bot = o2[b0:b0 + QROWS, :LANES] / (o2[b0:b0 + QROWS, LANES:] + e_scr[slot, b0:b0 + QROWS, :])
            cols = slice(g * LANES, (g + 1) * LANES)
            o_ref[rows, cols] = (jnp.where(low, top, bot) * sg_ref[rows, cols].astype(F32)).astype(BF16)

    return qk, sm, pv


def _attn_c_kernel(sink_ref, q_ref, k_ref, v_ref, sg_ref, o_ref, s_scr, p_scr, e_scr, *, n_steps):
    def geom(m):
        if isinstance(m, int) and m == 0:
            return 0, 0, QROWS, C_WIN - QROWS
        if isinstance(m, int):
            return m * QROWS, (m - 1) * QROWS, C_WIN, 0
        return pl.multiple_of(m * QROWS, QROWS), pl.multiple_of((m - 1) * QROWS, QROWS), C_WIN, 0

    _pipeline(n_steps, 1, 2, _attn_c_stages(sink_ref, q_ref, k_ref, v_ref, sg_ref, o_ref,
                                            s_scr, p_scr, e_scr, geom, None))


def _attn_c_window_kernel(sink_ref, q_ref, kt_ref, vt_ref, sg_ref, o_ref, *, n_valid):
    S = q_ref.shape[0]
    low = _low_half((S, LANES))
    pair = pl.program_id(0)
    j = lax.broadcasted_iota(jnp.int32, (S, C_WIN), 1)
    mask = jnp.where((j // CHUNK <= WINDOW_C // CHUNK) & (j < n_valid), 0.0, NEG_INF)
    blocks = [(hh, g) for hh in range(2) for g in range(G_C)]
    q2 = jnp.concatenate([_head_of_pair(q_ref[:, g * LANES:(g + 1) * LANES], low, hh) for hh, g in blocks], axis=0)
    z = jnp.dot(q2, kt_ref[...], preferred_element_type=F32)
    probs, sink_terms = [], []
    for bi, (hh, g) in enumerate(blocks):
        x = z[bi * S:(bi + 1) * S] + mask
        sink = sink_ref[(2 * pair + hh) * G_C + g] * LOG2E
        mx = jnp.maximum(jnp.max(x, axis=-1, keepdims=True), sink)
        probs.append(jnp.exp2(x - mx).astype(BF16))
        sink_terms.append(jnp.exp2(sink - mx))
    vt = vt_ref[...]
    o2 = _dot_nt(jnp.concatenate(probs, axis=0), jnp.concatenate([vt, jnp.ones_like(vt)], axis=0))
    for g in range(G_C):
        halves = []
        for bi in (g, G_C + g):
            blk = o2[bi * S:(bi + 1) * S]
            halves.append(blk[:, :LANES] / (blk[:, LANES:] + sink_terms[bi]))
        cols = slice(g * LANES, (g + 1) * LANES)
        o_ref[:, cols] = (jnp.where(low, *halves) * sg_ref[:, cols].astype(F32)).astype(BF16)


def _attn_c(q, k, v, sg, sinks, n_valid=None):
    B, Sq, _ = q.shape
    qw = W_C // (KVW_C // LANES)
    qspec = pl.BlockSpec((None, Sq, qw), lambda p, b: (b, 0, p))
    sspec = pl.BlockSpec(memory_space=pltpu.SMEM)
    stacked = 2 * G_C * QROWS
    if n_valid is None:
        kern = functools.partial(_attn_c_kernel, n_steps=Sq // QROWS)
        kspec = pl.BlockSpec((None, Sq, LANES), lambda p, b: (b, 0, p))
        scratch = [pltpu.VMEM((2, stacked, C_WIN), F32), pltpu.VMEM((2, stacked, C_WIN), BF16),
                   pltpu.VMEM((2, stacked, LANES), F32)]
    else:
        assert Sq <= CHUNK and k.shape[2] == C_WIN
        kern = functools.partial(_attn_c_window_kernel, n_valid=n_valid)
        kspec = pl.BlockSpec((None, LANES, C_WIN), lambda p, b: (b, p, 0))
        scratch = []
    return pl.pallas_call(
        kern,
        grid=(KVW_C // LANES, B),
        in_specs=[sspec, qspec, kspec, kspec, qspec],
        out_specs=qspec,
        out_shape=jax.ShapeDtypeStruct(q.shape, BF16),
        scratch_shapes=scratch,
        compiler_params=_params(2),
        name="attn_c" if n_valid is None else "attn_c_window",
    )(sinks, q, k, v, sg)


def _c_head_order(w, axis):
    lead, trail = w.shape[:axis], w.shape[axis + 1:]
    w = w.reshape(*lead, KV_C // 2, 2, G_C, HEAD_DIM, *trail)
    return jnp.swapaxes(w, axis + 1, axis + 2).reshape(*lead, W_C, *trail)


def _heads(x2d, B, h):
    return x2d.reshape(B, x2d.shape[0] // B, h, HEAD_DIM)


def _flush(act):
    if not isinstance(act, tuple):
        return act
    merge, B, S, tm = act
    tm = MERGE_ROWS if (B * S) % MERGE_ROWS == 0 else tm
    return _merge(*merge, tm).reshape(B, S, D_MODEL)


def _ab_layer(x, cache, g_pre, w_in, w_out, g_post, table):
    x = _flush(x)
    B, S, _ = x.shape
    T = B * S
    x2d = x.reshape(T, D_MODEL)
    r3 = lambda t: t.reshape(B, S, -1)

    def cached(t, rows):
        g, w, c = t.shape
        t = t.reshape(g, w // HEAD_DIM, HEAD_DIM, c // rows, rows)
        return jnp.transpose(t, (0, 3, 4, 1, 2)).reshape(B, rows, w // HEAD_DIM, HEAD_DIM)

    if cache is None:
        tm = min(S, PROJ_ROWS)
        la = min(LEFT_CHUNKS_A * CHUNK, S)
        assert S % tm == 0 and la <= tm
        proj = _proj_ab(x2d, g_pre[None, :], w_in.astype(BF16), tm, S // tm, la)
        qa, ka, va, sga, qb, kb, vb, sgb = proj[:8]
        oa = _attn_a(r3(qa), r3(ka), r3(va), r3(sga), _bias_table(table, A_WIN), window=False)
        ob = _attn_b(r3(qb), r3(kb), r3(vb), r3(sgb))
    else:
        ca_k, ca_v, cb_k, cb_v = cache
        la = ca_k.shape[1]
        assert la == LEFT_CHUNKS_A * CHUNK and S <= CHUNK and cb_k.shape[1] % SB_BLK == 0
        keys_minor = lambda c: jnp.transpose(c, (0, 2, 3, 1)).reshape(B, c.shape[2] * HEAD_DIM, c.shape[1])
        tm = T
        proj = _proj_ab(x2d, g_pre[None, :], w_in.astype(BF16), tm, 1, tm)
        qa, ka, va, sga, qb, kb, vb, sgb = proj[:8]

        def window(c, new):
            parts = [keys_minor(c).astype(BF16), jnp.swapaxes(r3(new), 1, 2)]
            return jnp.pad(jnp.concatenate(parts, axis=2), ((0, 0), (0, 0), (0, A_WIN - la - S)))

        oa = _attn_a(r3(qa), window(ca_k, ka), window(ca_v, va), r3(sga), _bias_table(table, la + S), window=True)
        ob = _attn_b_cached(r3(qb), r3(kb), r3(vb), keys_minor(cb_k), keys_minor(cb_v), r3(sgb))
    rows = (cached(proj[8], min(la, S)), cached(proj[9], min(la, S)), cached(proj[10], S), cached(proj[11], S))
    merge = ([oa.reshape(T, W_A), ob.reshape(T, W_B)], w_out.astype(BF16), g_post[None, :], x2d)
    return (merge, B, S, tm), rows


def _c_layer(x, cache, past, g_pre, w_in, sinks, w_out, g_post):
    if isinstance(x, tuple):
        merge, B, S, _ = x
        x2d = None
    else:
        merge, (B, S, _) = None, x.shape
        x2d = x.reshape(B * S, D_MODEL)
    T = B * S
    g0 = W_C + 2 * KVW_C
    w_perm = jnp.concatenate([_c_head_order(w_in[:, :W_C], 1), w_in[:, W_C:g0], _c_head_order(w_in[:, g0:], 1)],
                             axis=1)
    r3 = lambda t: t.reshape(B, S, -1)
    if cache is None:
        tm = min(S, PROJ_ROWS)
        lc = min(WINDOW_C, S)
        assert S % tm == 0 and lc <= tm
        cos, sin = _rope_tables(S, 0, S)
        proj = _proj_c(x2d, g_pre[None, :], w_perm.astype(BF16), cos, sin, tm, S // tm, lc, merge)
        x2d, (q, k, v, sg, kf, vf) = (proj[0], proj[1:]) if merge is not None else (x2d, proj)
        o = _attn_c(r3(q), r3(k), r3(v), r3(sg), sinks)
    else:
        tm = T
        cos, sin = _rope_tables(T, past, S)
        proj = _proj_c(x2d, g_pre[None, :], w_perm.astype(BF16), cos, sin, tm, 1, tm, merge)
        x2d, (q, k, v, sg, kf, vf) = (proj[0], proj[1:]) if merge is not None else (x2d, proj)
        cc_k, cc_v = cache
        lc = cc_k.shape[1]
        assert lc == WINDOW_C and S <= CHUNK

        def window(c, new):
            old = jnp.transpose(c, (0, 2, 3, 1)).reshape(B, KVW_C, lc).astype(BF16)
            parts = [old, jnp.swapaxes(r3(new), 1, 2)]
            return jnp.pad(jnp.concatenate(parts, axis=2), ((0, 0), (0, 0), (0, C_WIN - lc - S)))

        o = _attn_c(r3(q), window(cc_k, k), window(cc_v, v), r3(sg), sinks, n_valid=lc + S)
    rows = (_heads(kf, B, KV_C), _heads(vf, B, KV_C))
    merge = ([o.reshape(T, W_C)], _c_head_order(w_out, 0).astype(BF16), g_post[None, :], x2d)
    return (merge, B, S, tm), rows


def kernel(x_prompt, x_sample, cache_a_k, cache_a_v, cache_b_k, cache_b_v, cache_c_k, cache_c_v,
           ab_norm_pre, ab_w_in, ab_w_out, ab_norm_post, a_rel_bias,
           c_norm_pre, c_w_in, c_sinks, c_w_out, c_norm_post):
    past = cache_b_k.shape[2]
    depth = ab_w_in.shape[0] + c_w_in.shape[0]
    yp, ys = x_prompt, x_sample
    ab_p, ab_s, c_p, c_s = [], [], [], []
    for layer in range(depth):
        i = layer // 2
        if layer % 2 == 0:
            w = (ab_norm_pre[i], ab_w_in[i], ab_w_out[i], ab_norm_post[i], a_rel_bias[i])
            yp, rows = _ab_layer(yp, None, *w)
            ab_p.append(rows)
            ys, rows = _ab_layer(ys, (cache_a_k[i], cache_a_v[i], cache_b_k[i], cache_b_v[i]), *w)
            ab_s.append(rows)
        else:
            w = (c_norm_pre[i], c_w_in[i], c_sinks[i], c_w_out[i], c_norm_post[i])
            yp, rows = _c_layer(yp, None, past, *w)
            c_p.append(rows)
            ys, rows = _c_layer(ys, (cache_c_k[i], cache_c_v[i]), past, *w)
            c_s.append(rows)
    st = lambda rows, j: jnp.stack([r[j] for r in rows])
    return (_flush(yp), _flush(ys),
            st(ab_p, 0), st(ab_p, 1), st(ab_p, 2), st(ab_p, 3), st(c_p, 0), st(c_p, 1),
            st(ab_s, 0), st(ab_s, 1), st(ab_s, 2), st(ab_s, 3), st(c_s, 0), st(c_s, 1))
```

```python
import functools

import jax
import jax.numpy as jnp
from jax import lax
from jax.experimental import pallas as pl
from jax.experimental.pallas import tpu as pltpu

D_MODEL = 1024
CHUNK = 64
HEAD_DIM = 64
H_A = 8
H_B = 8
LEFT_CHUNKS_A = 8
REL_CLIP_A = 128
N_REL_A = 2 * REL_CLIP_A + 1
W_A = H_A * HEAD_DIM
W_B = H_B * HEAD_DIM
H_C = 16
KV_C = 4
G_C = H_C // KV_C
WINDOW_C = 128
W_C = H_C * HEAD_DIM
KVW_C = KV_C * HEAD_DIM
ROPE_THETA = 10000.0
RMS_EPS = 1e-6
NEG_INF = -1e30
SCALE = HEAD_DIM ** -0.5

LANES = 128
QROWS = 128
A_WIN = (LEFT_CHUNKS_A + 2) * CHUNK
A_TAB = 384
A_ROLLW = A_WIN + QROWS
C_WIN = WINDOW_C + QROWS
MERGE_ROWS = 2048
STRIP = 32
LOG2E = 1.4426950408889634
SB_BLK = 256
SB_DEAD = 104.0 * LOG2E
VMEM_LIMIT = 56 * 1024 * 1024
PROJ_ROWS = 1024

F32 = jnp.float32
BF16 = jnp.bfloat16


def _params(n_axes, semantics="parallel"):
    return pltpu.CompilerParams(dimension_semantics=(semantics,) * n_axes,
                                vmem_limit_bytes=VMEM_LIMIT)


def _rms_scale(x):
    return lax.rsqrt(jnp.mean(x * x, axis=-1, keepdims=True) + RMS_EPS)


def _silu(u):
    return u / (1.0 + jnp.exp(-u))


def _low_half(shape):
    return lax.broadcasted_iota(jnp.int32, shape, 1) < HEAD_DIM


def _head_of_pair(x_bf16, low, h):
    xf = x_bf16.astype(F32)
    return jnp.where(low if h == 0 else jnp.logical_not(low), xf, 0.0).astype(BF16)


def _dot_nt(a, b):
    return lax.dot_general(a, b, (((1,), (1,)), ((), ())), preferred_element_type=F32)


def _proj_ab_kernel(x_ref, g_ref, w_ref, qa, ka, va, sga, qb, kb, vb, sgb, kaf, vaf, kbf, vbf,
                    *, tiles_per_seq, keep_a):
    tm = x_ref.shape[0]

    def body(with_tails):
        x = x_ref[...]
        xn = (x * _rms_scale(x) * g_ref[...]).astype(BF16)

        def col(j):
            return jnp.dot(xn, w_ref[:, j * W_A:(j + 1) * W_A], preferred_element_type=F32)

        sga[...] = _silu(col(3)).astype(BF16)
        sgb[...] = _silu(col(7)).astype(BF16)
        u = col(5); kb[...] = u.astype(BF16); kbf[...] = u.T
        u = col(6); vb[...] = u.astype(BF16); vbf[...] = u.T
        u = col(1); ka[...] = u.astype(BF16)
        if with_tails:
            kaf[...] = u[tm - keep_a:, :].T
        u = col(2); va[...] = u.astype(BF16)
        if with_tails:
            vaf[...] = u[tm - keep_a:, :].T
        qa[...] = (col(0) * (SCALE * LOG2E)).astype(BF16)
        qb[...] = (col(4) * (SCALE * LOG2E)).astype(BF16)

    if tiles_per_seq == 1:
        body(True)
    else:
        last = pl.program_id(0) % tiles_per_seq == tiles_per_seq - 1
        pl.when(last)(functools.partial(body, True))
        pl.when(jnp.logical_not(last))(functools.partial(body, False))


def _proj_ab(x2d, g_pre, w_bf16, tm, tiles_per_seq, keep_a):
    T = x2d.shape[0]
    n_tiles = T // tm
    row = lambda w: pl.BlockSpec((tm, w), lambda i: (i, 0))
    const = lambda a: pl.BlockSpec(a.shape, lambda i: (0, 0), pipeline_mode=pl.Buffered(1))
    bshape = jax.ShapeDtypeStruct((T, W_A), BF16)
    groups = n_tiles // tiles_per_seq
    a_spec = pl.BlockSpec((None, W_A, keep_a), lambda i: (i // tiles_per_seq, 0, 0))
    a_shape = jax.ShapeDtypeStruct((groups, W_A, keep_a), F32)
    b_spec = pl.BlockSpec((None, W_B, tm), lambda i: (i // tiles_per_seq, 0, i % tiles_per_seq))
    b_shape = jax.ShapeDtypeStruct((groups, W_B, tiles_per_seq * tm), F32)
    return pl.pallas_call(
        functools.partial(_proj_ab_kernel, tiles_per_seq=tiles_per_seq, keep_a=keep_a),
        grid=(n_tiles,),
        in_specs=[row(D_MODEL), const(g_pre), const(w_bf16)],
        out_specs=[row(W_A)] * 8 + [a_spec, a_spec, b_spec, b_spec],
        out_shape=[bshape] * 8 + [a_shape, a_shape, b_shape, b_shape],
        compiler_params=_params(1, "arbitrary"),
        name="proj_ab",
    )(x2d, g_pre, w_bf16)


def _merge_body(h_refs, w_ref, g_ref, x_ref, y_ref):
    acc = None
    off = 0
    for h in h_refs:
        k = h.shape[1]
        d = jnp.dot(h[...], w_ref[off:off + k, :], preferred_element_type=F32)
        acc = d if acc is None else acc + d
        off += k
    y = x_ref[...] + acc * _rms_scale(acc) * g_ref[...]
    y_ref[...] = y
    return y


def _merge_kernel(*refs, n_parts):
    _merge_body(refs[:n_parts], *refs[n_parts:])


def _merge(parts, w_bf16, g_post, x2d, tm):
    T = x2d.shape[0]
    row = lambda w: pl.BlockSpec((tm, w), lambda i: (i, 0))
    full = lambda a: pl.BlockSpec(a.shape, lambda i: (0, 0))
    return pl.pallas_call(
        functools.partial(_merge_kernel, n_parts=len(parts)),
        grid=(T // tm,),
        in_specs=[row(p.shape[1]) for p in parts] + [full(w_bf16), full(g_post), row(D_MODEL)],
        out_specs=row(D_MODEL),
        out_shape=jax.ShapeDtypeStruct((T, D_MODEL), F32),
        compiler_params=_params(1),
        name="merge",
    )(*parts, w_bf16, g_post, x2d)


def _bias_kernel(tab_ref, out_ref, *, n_valid):
    m = lax.broadcasted_iota(jnp.int32, (A_TAB, A_ROLLW), 1)
    t = lax.broadcasted_iota(jnp.int32, (A_TAB, A_ROLLW), 0)
    idx = jnp.where(m <= A_WIN, jnp.clip(A_WIN - QROWS - m, -REL_CLIP_A, REL_CLIP_A) + REL_CLIP_A,
                    2 * REL_CLIP_A)
    onehot = jnp.where(t == idx, 1.0, 0.0).astype(BF16)
    tab = tab_ref[...]
    hi = tab.astype(BF16)
    r1 = tab - hi.astype(F32)
    mid = r1.astype(BF16)
    lo = (r1 - mid.astype(F32)).astype(BF16)
    src = (jnp.dot(hi, onehot, preferred_element_type=F32)
           + jnp.dot(mid, onehot, preferred_element_type=F32)
           + jnp.dot(lo, onehot, preferred_element_type=F32))
    i = lax.broadcasted_iota(jnp.int32, (QROWS, A_WIN), 0)
    j = lax.broadcasted_iota(jnp.int32, (QROWS, A_WIN), 1)
    qc = LEFT_CHUNKS_A + i // CHUNK
    kc = j // CHUNK
    valid = (kc <= qc) & (kc >= qc - LEFT_CHUNKS_A) & (j < n_valid)
    for h in range(H_A):
        row = jnp.broadcast_to(src[h:h + 1, :], (QROWS, A_ROLLW))
        toep = pltpu.roll(row, 0, 1, stride=1, stride_axis=0)
        out_ref[h * QROWS:(h + 1) * QROWS, :] = jnp.where(valid, toep[:, :A_WIN] * LOG2E, NEG_INF)


def _bias_table(table, n_valid):
    tab = jnp.pad(table, ((0, 0), (0, A_TAB - N_REL_A)))
    return pl.pallas_call(
        functools.partial(_bias_kernel, n_valid=n_valid),
        out_shape=jax.ShapeDtypeStruct((H_A * QROWS, A_WIN), F32),
        name="a_bias",
    )(tab)


def _pipeline(n_units, n_static, period, stages):
    depth = len(stages)

    def tick(t, tmod):
        for lag in reversed(range(depth)):
            u = t - lag
            if not isinstance(u, int) or 0 <= u < n_units:
                stages[lag](u, (tmod - lag) % period)

    first = n_static + depth - 1
    for t in range(min(first, n_units + depth - 1)):
        tick(t, t % period)
    groups = max(n_units - first, 0) // period

    def body(g, c):
        t = first + period * g
        for d in range(period):
            tick(t + d, (first + d) % period)
        return c

    if groups:
        lax.fori_loop(0, groups, body, 0)
    for t in range(max(first, first + period * groups), n_units + depth - 1):
        tick(t, t % period)


def _softmax_strips(s_scr, p_scr, width):
    for r in range(0, s_scr.shape[0], STRIP):
        x = s_scr[r:r + STRIP, :width]
        p_scr[r:r + STRIP, :width] = jnp.exp2(x - jnp.max(x, axis=-1, keepdims=True)).astype(BF16)


def _attn_a_stages(q_ref, k_ref, v_ref, sg_ref, bias_ref, o_ref, s_scr, p_scr, geom):
    low = _low_half((QROWS, LANES))

    def qk(m, slot):
        q0, k0, width, bcol0 = geom(m)
        qm = q_ref[pl.ds(q0, QROWS), :]
        q2 = jnp.concatenate([_head_of_pair(qm, low, 0), _head_of_pair(qm, low, 1)], axis=0)
        s_scr[slot, :, :width] = _dot_nt(q2, k_ref[pl.ds(k0, width), :]) + bias_ref[:, bcol0:bcol0 + width]

    def sm(m, slot):
        _softmax_strips(s_scr.at[slot], p_scr.at[slot], geom(m)[2])

    def pv(m, slot):
        q0, k0, width, _ = geom(m)
        vw = v_ref[pl.ds(k0, width), :]
        o2 = jnp.dot(p_scr[slot, :, :width], jnp.concatenate([vw, jnp.ones_like(vw)], axis=1),
                     preferred_element_type=F32)
        top, bot = o2[:QROWS], o2[QROWS:]
        o = jnp.where(low, top[:, :LANES] / top[:, LANES:], bot[:, :LANES] / bot[:, LANES:])
        o_ref[pl.ds(q0, QROWS), :] = (o * sg_ref[pl.ds(q0, QROWS), :].astype(F32)).astype(BF16)

    return qk, sm, pv


def _attn_a_kernel(q_ref, k_ref, v_ref, sg_ref, bias_ref, o_ref, s_scr, p_scr, *, n_steps):
    lead = A_WIN // QROWS - 1

    def geom(m):
        if isinstance(m, int) and m < lead:
            return m * QROWS, 0, (m + 1) * QROWS, (lead - m) * QROWS
        if isinstance(m, int):
            return m * QROWS, (m - lead) * QROWS, A_WIN, 0
        return pl.multiple_of(m * QROWS, QROWS), pl.multiple_of((m - lead) * QROWS, QROWS), A_WIN, 0

    _pipeline(n_steps, lead, 2, _attn_a_stages(q_ref, k_ref, v_ref, sg_ref, bias_ref, o_ref, s_scr, p_scr, geom))


def _attn_a_window_kernel(q_ref, kt_ref, vt_ref, sg_ref, bias_ref, o_ref, s_scr, p_scr):
    S = q_ref.shape[0]
    low = _low_half((S, LANES))
    for p in range(W_A // LANES):
        cols = slice(p * LANES, (p + 1) * LANES)
        qm = q_ref[:, cols]
        q2 = jnp.concatenate([_head_of_pair(qm, low, 0), _head_of_pair(qm, low, 1)], axis=0)
        b0 = 2 * p * QROWS
        bias = jnp.concatenate([bias_ref[b0:b0 + S, :], bias_ref[b0 + QROWS:b0 + QROWS + S, :]], axis=0)
        s_view, p_view = s_scr.at[p % 2, pl.ds(0, 2 * S)], p_scr.at[p % 2, pl.ds(0, 2 * S)]
        s_view[...] = jnp.dot(q2, kt_ref[cols, :], preferred_element_type=F32) + bias
        _softmax_strips(s_view, p_view, A_WIN)
        vt = vt_ref[cols, :]
        o2 = _dot_nt(p_view[...], jnp.concatenate([vt, jnp.ones_like(vt)], axis=0))
        o = jnp.where(low, o2[:S, :LANES] / o2[:S, LANES:], o2[S:, :LANES] / o2[S:, LANES:])
        o_ref[:, cols] = (o * sg_ref[:, cols].astype(F32)).astype(BF16)


def _attn_a(q, k, v, sg, bias, window):
    B, Sq, _ = q.shape
    if window:
        assert Sq % STRIP == 0 and Sq <= CHUNK and k.shape[2] == A_WIN
        grid = (B,)
        qspec = pl.BlockSpec((None, Sq, W_A), lambda b: (b, 0, 0))
        kspec = pl.BlockSpec((None, W_A, A_WIN), lambda b: (b, 0, 0))
        bspec = pl.BlockSpec(bias.shape, lambda b: (0, 0))
        kern = _attn_a_window_kernel
    else:
        grid = (W_A // LANES, B)
        qspec = kspec = pl.BlockSpec((None, Sq, LANES), lambda p, b: (b, 0, p))
        bspec = pl.BlockSpec((2 * QROWS, A_WIN), lambda p, b: (p, 0))
        kern = functools.partial(_attn_a_kernel, n_steps=Sq // QROWS)
    return pl.pallas_call(
        kern,
        grid=grid,
        in_specs=[qspec, kspec, kspec, qspec, bspec],
        out_specs=qspec,
        out_shape=jax.ShapeDtypeStruct(q.shape, BF16),
        scratch_shapes=[pltpu.VMEM((2, 2 * QROWS, A_WIN), F32), pltpu.VMEM((2, 2 * QROWS, A_WIN), BF16)],
        compiler_params=_params(len(grid)),
        name="attn_a_window" if window else "attn_a",
    )(q, k, v, sg, bias)


def _sb_stages(scr, unit, past_transposed=False, qrows=SB_BLK, diag_w=SB_BLK):
    z_scr, h_scr, a_scr, rs_scr = scr
    rows = 2 * qrows
    low = _low_half((qrows, LANES))
    before = {w: lax.broadcasted_iota(jnp.int32, (STRIP, w), 1) - lax.broadcasted_iota(jnp.int32, (STRIP, w), 0)
              for w in {LANES, SB_BLK, diag_w}}
    tri = {}
    for kw in {SB_BLK, diag_w}:
        r = lax.broadcasted_iota(jnp.int32, (kw, kw), 0)
        c = lax.broadcasted_iota(jnp.int32, (kw, kw), 1)
        tri[kw] = jnp.where(r > c, 1.0, 0.0).astype(BF16)
    keys = lambda diag: diag_w if diag else SB_BLK

    def scores(u, us):
        q_ref, q0, k_ref, _, k0, diag = unit(u, us)[:6]
        qm = q_ref[pl.ds(q0, qrows), :]
        q2 = jnp.concatenate([_head_of_pair(qm, low, 0), _head_of_pair(qm, low, 1)], axis=0)
        kw = keys(diag)
        if past_transposed and not diag:
            z_scr[us % 4, :, :kw] = jnp.dot(q2, k_ref[:, pl.ds(k0, kw)].astype(BF16), preferred_element_type=F32)
        else:
            z_scr[us % 4, :, :kw] = _dot_nt(q2, k_ref[pl.ds(k0, kw), :].astype(BF16))

    def width(diag, r0):
        return LANES if diag and r0 % qrows + STRIP <= LANES else keys(diag)

    def terms(u, us):
        diag = unit(u, us)[5]
        zs, hs, rs = z_scr.at[us % 4], h_scr.at[us % 2], rs_scr.at[us % 2]
        for r0 in range(0, rows, STRIP):
            w = width(diag, r0)
            z = zs[r0:r0 + STRIP, :w]
            sp = jnp.maximum(z, 0.0) + jnp.log2(1.0 + jnp.exp2(-jnp.abs(z)))
            zs[r0:r0 + STRIP, :w] = z - sp
            if diag:
                sp = jnp.where(before[w] < r0 % qrows, sp, 0.0)
            hs[r0:r0 + STRIP, :w] = sp.astype(BF16)
            if w < keys(diag):
                hs[r0:r0 + STRIP, w:keys(diag)] = jnp.zeros((STRIP, keys(diag) - w), BF16)
            rs[r0:r0 + STRIP, :] = jnp.broadcast_to(jnp.sum(sp, axis=-1, keepdims=True), (STRIP, LANES))

    def weights(u, us):
        diag, dec = unit(u, us)[5:7]
        zs, rs, a_s = z_scr.at[us % 4], rs_scr.at[us % 2], a_scr.at[us % 2]
        kw = keys(diag)
        inner = jnp.dot(h_scr[us % 2, :, :kw], tri[kw], preferred_element_type=F32)
        for r0 in range(0, rows, STRIP):
            w = width(diag, r0)
            x = zs[r0:r0 + STRIP, :w] - inner[r0:r0 + STRIP, :w]
            if diag:
                dec[r0:r0 + STRIP, :] = rs[r0:r0 + STRIP, :]
            else:
                d = dec[r0:r0 + STRIP, :]
                x = x - jnp.concatenate([d, d], axis=1)
                dec[r0:r0 + STRIP, :] = d + rs[r0:r0 + STRIP, :]
            a = jnp.exp2(x)
            if diag:
                a = jnp.where(before[w] < r0 % qrows, a, 0.0)
            a_s[r0:r0 + STRIP, :w] = a.astype(BF16)
            if w < kw:
                a_s[r0:r0 + STRIP, w:kw] = jnp.zeros((STRIP, kw - w), BF16)

    def values(u, us):
        _, _, _, v_ref, k0, diag, _, acc = unit(u, us)[:8]
        kw = keys(diag)
        if past_transposed and not diag:
            pv = _dot_nt(a_scr[us % 2, :, :kw], v_ref[:, pl.ds(k0, kw)].astype(BF16))
        else:
            pv = jnp.dot(a_scr[us % 2, :, :kw], v_ref[pl.ds(k0, kw), :].astype(BF16),
                         preferred_element_type=F32)
        if diag:
            acc[...] = pv
        else:
            acc[...] += pv
        if len(unit(u, us)) > 8:
            unit(u, us)[8]()

    return scores, terms, weights, values


def _sb_live(dec):
    return (jnp.min(dec[...]) < SB_DEAD).astype(jnp.int32)


def _sb_deeper(stages, j0, dec, fetch=None):
    def body(c):
        if fetch is not None:
            fetch(c[0])
        for stage in stages:
            stage(c[0], 0)
        return c[0] - 1, _sb_live(dec)

    lax.while_loop(lambda c: (c[0] >= 0) & (c[1] > 0), body, (jnp.asarray(j0, jnp.int32), jnp.int32(1)))


def _sb_scratch(n_qblocks, qrows=SB_BLK):
    rows = 2 * qrows
    return [pltpu.VMEM((4, rows, SB_BLK), F32), pltpu.VMEM((2, rows, SB_BLK), BF16),
            pltpu.VMEM((2, rows, SB_BLK), BF16), pltpu.VMEM((2, rows, LANES), F32),
            pltpu.VMEM((n_qblocks, rows, LANES), F32), pltpu.VMEM((n_qblocks, rows, LANES), F32),
            pltpu.SMEM((n_qblocks,), jnp.int32)]


def _sb_output(acc, sg):
    qrows = sg.shape[0]
    low = _low_half((qrows, LANES))
    return (jnp.where(low, acc[:qrows, :], acc[qrows:, :]) * sg.astype(F32)).astype(BF16)


def _attn_b_kernel(q_ref, k_ref, v_ref, sg_ref, o_ref, z_scr, h_scr, a_scr, rs_scr, dec_all, acc_all, live):
    n_q = q_ref.shape[0] // SB_BLK
    scr = (z_scr, h_scr, a_scr, rs_scr)
    blk = lambda i: i * SB_BLK if isinstance(i, int) else pl.multiple_of(i * SB_BLK, SB_BLK)

    def emit(i):
        rows = pl.ds(blk(i), SB_BLK)
        o_ref[rows, :] = _sb_output(acc_all[i], sg_ref[rows, :])

    def unit(u, us):
        i = (u + 1) // 2
        diag = u % 2 == 1 or u == 0 if isinstance(u, int) else us % 2 == 1
        base = (q_ref, blk(i), k_ref, v_ref, blk(i if diag else i - 1), diag, dec_all.at[i], acc_all.at[i])
        if diag and not (isinstance(u, int) and u == 0):
            return base

        def done():
            emit(i)
            live[i] = _sb_live(dec_all.at[i])

        return base + (done,)

    _pipeline(2 * n_q - 1, 1, 4, _sb_stages(scr, unit))

    def finish(i, c):
        @pl.when(live[i] > 0)
        def _():
            unit_at = lambda j, us: (q_ref, blk(i), k_ref, v_ref, blk(j), False, dec_all.at[i], acc_all.at[i])
            _sb_deeper(_sb_stages(scr, unit_at), i - 2, dec_all.at[i])
            emit(i)

        return c

    lax.fori_loop(2, n_q, finish, 0)


def _attn_b_cached_kernel(q_ref, kd_ref, vd_ref, kl_ref, vl_ref, kp_hbm, vp_hbm, sg_ref, o_ref,
                          z_scr, h_scr, a_scr, rs_scr, dec_all, acc_all, live, kd_pad, vd_pad, kbuf, vbuf, sem):
    n_past = kp_hbm.shape[2] // SB_BLK
    pair, batch = pl.program_id(0), pl.program_id(1)
    scr = (z_scr, h_scr, a_scr, rs_scr)
    state = (dec_all.at[0], acc_all.at[0])
    qrows = q_ref.shape[0]
    shape = dict(past_transposed=True, qrows=qrows, diag_w=LANES)
    for pad, src in ((kd_pad, kd_ref), (vd_pad, vd_ref)):
        pad[...] = jnp.zeros(pad.shape, BF16)
        pad[:qrows, :] = src[...]

    def emit():
        o_ref[...] = _sb_output(acc_all[0], sg_ref[...])

    def done():
        emit()
        live[0] = _sb_live(dec_all.at[0])

    def unit(u, us):
        if u == 0:
            return (q_ref, 0, kd_pad, vd_pad, 0, True) + state
        return (q_ref, 0, kl_ref, vl_ref, 0, False) + state + (done,)

    _pipeline(2, 2, 4, _sb_stages(scr, unit, **shape))

    @pl.when(live[0] > 0)
    def _():
        def fetch(j):
            rows = pl.ds(pl.multiple_of(pair * LANES, LANES), LANES)
            cols = pl.ds(pl.multiple_of(j * SB_BLK, SB_BLK), SB_BLK)
            copies = [pltpu.make_async_copy(hbm.at[batch, rows, cols], buf, sem.at[n])
                      for n, (hbm, buf) in enumerate(((kp_hbm, kbuf), (vp_hbm, vbuf)))]
            for c in copies:
                c.start()
            for c in copies:
                c.wait()

        unit_at = lambda j, us: (q_ref, 0, kbuf, vbuf, 0, False) + state
        _sb_deeper(_sb_stages(scr, unit_at, **shape), n_past - 2, dec_all.at[0], fetch)
        emit()


def _attn_b(q, k, v, sg):
    B, S, _ = q.shape
    spec = pl.BlockSpec((None, S, LANES), lambda p, b: (b, 0, p))
    return pl.pallas_call(
        _attn_b_kernel,
        grid=(W_B // LANES, B),
        in_specs=[spec] * 4,
        out_specs=spec,
        out_shape=jax.ShapeDtypeStruct(q.shape, BF16),
        scratch_shapes=_sb_scratch(S // SB_BLK),
        compiler_params=_params(2),
        name="attn_b",
    )(q, k, v, sg)


def _attn_b_cached(q, kd, vd, kp, vp, sg):
    B, S, _ = q.shape
    P = kp.shape[2]
    assert S % STRIP == 0 and S <= LANES and P % SB_BLK == 0
    spec = pl.BlockSpec((None, S, LANES), lambda p, b: (b, 0, p))
    last = pl.BlockSpec((None, LANES, SB_BLK), lambda p, b: (b, p, P // SB_BLK - 1))
    hbm = pl.BlockSpec(memory_space=pl.ANY)
    return pl.pallas_call(
        _attn_b_cached_kernel,
        grid=(W_B // LANES, B),
        in_specs=[spec, spec, spec, last, last, hbm, hbm, spec],
        out_specs=spec,
        out_shape=jax.ShapeDtypeStruct(q.shape, BF16),
        scratch_shapes=(_sb_scratch(1, S) + [pltpu.VMEM((LANES, LANES), BF16)] * 2
                        + [pltpu.VMEM((LANES, SB_BLK), F32)] * 2 + [pltpu.SemaphoreType.DMA((2,))]),
        compiler_params=_params(2),
        name="attn_b_cached",
    )(q, kd, vd, kp, vp, kp, vp, sg)


def _rope_table_kernel(inv_ref, cos_ref, sin_ref, *, pos0, period):
    shape = cos_ref.shape
    row = lax.broadcasted_iota(jnp.int32, shape, 0)
    lane = lax.broadcasted_iota(jnp.int32, shape, 1)
    ang = (pos0 + row % period).astype(F32) * inv_ref[...]
    cos_ref[...] = jnp.cos(ang)
    s = jnp.sin(ang)
    sin_ref[...] = jnp.where(lane % HEAD_DIM < HEAD_DIM // 2, -s, s)


def _rope_tables(n_rows, pos0, period):
    half = HEAD_DIM // 2
    inv = ROPE_THETA ** (-jnp.arange(half, dtype=F32) * (2.0 / HEAD_DIM))
    inv = jnp.tile(inv, LANES // half)[None, :]
    shape = jax.ShapeDtypeStruct((n_rows, LANES), F32)
    return pl.pallas_call(
        functools.partial(_rope_table_kernel, pos0=pos0, period=period),
        out_shape=[shape, shape],
        name="rope_table",
    )(inv)


def _proj_c_kernel(*refs, n_parts):
    if n_parts:
        w_out, g_post, x_ref, g_ref, w_ref, cos_ref, sin_ref, y_ref = refs[n_parts:n_parts + 8]
        x = _merge_body(refs[:n_parts], w_out, g_post, x_ref, y_ref)
        _proj_c_body(x, g_ref, w_ref, cos_ref, sin_ref, *refs[n_parts + 8:])
    else:
        _proj_c_body(refs[0][...], *refs[1:])


def _proj_c_body(x, g_ref, w_ref, cos_ref, sin_ref, q_o, k_o, v_o, sg_o, kf_o, vf_o):
    xn = (x * _rms_scale(x) * g_ref[...]).astype(BF16)
    tm = x.shape[0]
    keep = kf_o.shape[0]

    def keep_tail(o_ref, u):
        o_ref[...] = u[tm - keep:, :]

    cos = cos_ref[...]
    sin = sin_ref[...]
    first = lax.broadcasted_iota(jnp.int32, cos.shape, 1) % HEAD_DIM < HEAD_DIM // 2

    def rope(u):
        partner = jnp.where(first, pltpu.roll(u, LANES - HEAD_DIM // 2, 1), pltpu.roll(u, HEAD_DIM // 2, 1))
        return u * cos + partner * sin

    blk = 4 * LANES
    for c0 in range(0, W_C, blk):
        u = jnp.dot(xn, w_ref[:, c0:c0 + blk], preferred_element_type=F32)
        for j in range(0, blk, LANES):
            q_o[:, c0 + j:c0 + j + LANES] = (rope(u[:, j:j + LANES]) * (SCALE * LOG2E)).astype(BF16)
    u = jnp.dot(xn, w_ref[:, W_C:W_C + KVW_C], preferred_element_type=F32)
    r = jnp.concatenate([rope(u[:, j:j + LANES]) for j in range(0, KVW_C, LANES)], axis=1)
    k_o[...] = r.astype(BF16)
    keep_tail(kf_o, r)
    u = jnp.dot(xn, w_ref[:, W_C + KVW_C:W_C + 2 * KVW_C], preferred_element_type=F32)
    v_o[...] = u.astype(BF16)
    keep_tail(vf_o, u)
    g0 = W_C + 2 * KVW_C
    for c0 in range(0, W_C, blk):
        u = jnp.dot(xn, w_ref[:, g0 + c0:g0 + c0 + blk], preferred_element_type=F32)
        sg_o[:, c0:c0 + blk] = _silu(u).astype(BF16)


def _proj_c(x2d, g_pre, w_bf16, cos, sin, tm, tiles_per_seq, keep, merge=None):
    parts, w_out, g_post, x_res = merge if merge is not None else ((), None, None, None)
    T = (x_res if merge is not None else x2d).shape[0]
    n_tiles = T // tm
    nt = cos.shape[0] // tm
    row = lambda w: pl.BlockSpec((tm, w), lambda i: (i, 0))
    const = lambda a: pl.BlockSpec(a.shape, lambda i: (0, 0), pipeline_mode=pl.Buffered(1))
    tab = pl.BlockSpec((tm, LANES), lambda i: (i % nt, 0))
    tail = pl.BlockSpec((keep, KVW_C), lambda i: (i // tiles_per_seq, 0))
    sd = jax.ShapeDtypeStruct
    tail_shape = sd((n_tiles // tiles_per_seq * keep, KVW_C), F32)
    if merge is not None:
        operands = [*parts, w_out, g_post, x_res]
        in_specs = [row(p.shape[1]) for p in parts] + [const(w_out), const(g_post), row(D_MODEL)]
        out_specs, out_shape = [row(D_MODEL)], [sd((T, D_MODEL), F32)]
    else:
        operands, in_specs, out_specs, out_shape = [x2d], [row(D_MODEL)], [], []
    return pl.pallas_call(
        functools.partial(_proj_c_kernel, n_parts=len(parts)),
        grid=(n_tiles,),
        in_specs=in_specs + [const(g_pre), const(w_bf16), tab, tab],
        out_specs=out_specs + [row(W_C), row(KVW_C), row(KVW_C), row(W_C), tail, tail],
        out_shape=out_shape + [sd((T, W_C), BF16), sd((T, KVW_C), BF16), sd((T, KVW_C), BF16),
                               sd((T, W_C), BF16), tail_shape, tail_shape],
        compiler_params=_params(1, "arbitrary"),
        name="merge_proj_c" if merge is not None else "proj_c",
    )(*operands, g_pre, w_bf16, cos, sin)


def _attn_c_stages(sink_ref, q_ref, k_ref, v_ref, sg_ref, o_ref, s_scr, p_scr, e_scr, geom, n_valid):
    low = _low_half((QROWS, LANES))
    blocks = [(hh, g) for hh in range(2) for g in range(G_C)]

    def qk(m, slot):
        q0, k0, width, mcol0 = geom(m)
        i = lax.broadcasted_iota(jnp.int32, (QROWS, width), 0) // CHUNK
        j = lax.broadcasted_iota(jnp.int32, (QROWS, width), 1)
        jc = (j + mcol0) // CHUNK
        valid = (jc <= i + WINDOW_C // CHUNK) & (jc >= i)
        if n_valid is not None:
            valid = valid & (j < n_valid)
        mask = jnp.where(valid, 0.0, NEG_INF)
        rows = pl.ds(q0, QROWS)
        q2 = jnp.concatenate([_head_of_pair(q_ref[rows, g * LANES:(g + 1) * LANES], low, hh)
                              for hh, g in blocks], axis=0)
        z = _dot_nt(q2, k_ref[pl.ds(k0, width), :])
        for bi in range(len(blocks)):
            s_scr[slot, bi * QROWS:(bi + 1) * QROWS, :width] = z[bi * QROWS:(bi + 1) * QROWS] + mask

    def sm(m, slot):
        width = geom(m)[2]
        pair = pl.program_id(0)
        for bi, (hh, g) in enumerate(blocks):
            sink = sink_ref[(2 * pair + hh) * G_C + g] * LOG2E
            for r in range(bi * QROWS, (bi + 1) * QROWS, STRIP):
                x = s_scr[slot, r:r + STRIP, :width]
                mx = jnp.maximum(jnp.max(x, axis=-1, keepdims=True), sink)
                p_scr[slot, r:r + STRIP, :width] = jnp.exp2(x - mx).astype(BF16)
                e_scr[slot, r:r + STRIP, :] = jnp.broadcast_to(jnp.exp2(sink - mx), (STRIP, LANES))

    def pv(m, slot):
        q0, k0, width, _ = geom(m)
        rows = pl.ds(q0, QROWS)
        vw = v_ref[pl.ds(k0, width), :]
        o2 = jnp.dot(p_scr[slot, :, :width], jnp.concatenate([vw, jnp.ones_like(vw)], axis=1),
                     preferred_element_type=F32)
        for g in range(G_C):
            t0, b0 = g * QROWS, (G_C + g) * QROWS
            top = o2[t0:t0 + QROWS, :LANES] / (o2[t0:t0 + QROWS, LANES:] + e_scr[slot, t0:t0 + QROWS, :])
            bot = o2[b0:b0 + QROWS, :LANES] / (o2[b0:b0 + QROWS, LANES:] + e_scr[slot, b0:b0 + QROWS, :])
            cols = slice(g * LANES, (g + 1) * LANES)
            o_ref[rows, cols] = (jnp.where(low, top, bot) * sg_ref[rows, cols].astype(F32)).astype(BF16)

    return qk, sm, pv


def _attn_c_kernel(sink_ref, q_ref, k_ref, v_ref, sg_ref, o_ref, s_scr, p_scr, e_scr, *, n_steps):
    def geom(m):
        if isinstance(m, int) and m == 0:
            return 0, 0, QROWS, C_WIN - QROWS
        if isinstance(m, int):
            return m * QROWS, (m - 1) * QROWS, C_WIN, 0
        return pl.multiple_of(m * QROWS, QROWS), pl.multiple_of((m - 1) * QROWS, QROWS), C_WIN, 0

    _pipeline(n_steps, 1, 2, _attn_c_stages(sink_ref, q_ref, k_ref, v_ref, sg_ref, o_ref,
                                            s_scr, p_scr, e_scr, geom, None))


def _attn_c_window_kernel(sink_ref, q_ref, kt_ref, vt_ref, sg_ref, o_ref, *, n_valid):
    for pair in range(KVW_C // LANES):
        qw = W_C // (KVW_C // LANES)
        kv = slice(pair * LANES, (pair + 1) * LANES)
        _attn_c_window_pair(sink_ref, q_ref.at[:, pl.ds(pair * qw, qw)], kt_ref.at[kv, :], vt_ref.at[kv, :],
                            sg_ref.at[:, pl.ds(pair * qw, qw)], o_ref.at[:, pl.ds(pair * qw, qw)], pair, n_valid)


def _attn_c_window_pair(sink_ref, q_ref, kt_ref, vt_ref, sg_ref, o_ref, pair, n_valid):
    S = q_ref.shape[0]
    low = _low_half((S, LANES))
    j = lax.broadcasted_iota(jnp.int32, (S, C_WIN), 1)
    mask = jnp.where((j // CHUNK <= WINDOW_C // CHUNK) & (j < n_valid), 0.0, NEG_INF)
    blocks = [(hh, g) for hh in range(2) for g in range(G_C)]
    q2 = jnp.concatenate([_head_of_pair(q_ref[:, g * LANES:(g + 1) * LANES], low, hh) for hh, g in blocks], axis=0)
    z = jnp.dot(q2, kt_ref[...], preferred_element_type=F32)
    probs, sink_terms = [], []
    for bi, (hh, g) in enumerate(blocks):
        x = z[bi * S:(bi + 1) * S] + mask
        sink = sink_ref[(2 * pair + hh) * G_C + g] * LOG2E
        mx = jnp.maximum(jnp.max(x, axis=-1, keepdims=True), sink)
        probs.append(jnp.exp2(x - mx).astype(BF16))
        sink_terms.append(jnp.exp2(sink - mx))
    vt = vt_ref[...]
    o2 = _dot_nt(jnp.concatenate(probs, axis=0), jnp.concatenate([vt, jnp.ones_like(vt)], axis=0))
    for g in range(G_C):
        halves = []
        for bi in (g, G_C + g):
            blk = o2[bi * S:(bi + 1) * S]
            halves.append(blk[:, :LANES] / (blk[:, LANES:] + sink_terms[bi]))
        cols = slice(g * LANES, (g + 1) * LANES)
        o_ref[:, cols] = (jnp.where(low, *halves) * sg_ref[:, cols].astype(F32)).astype(BF16)


def _attn_c(q, k, v, sg, sinks, n_valid=None):
    B, Sq, _ = q.shape
    sspec = pl.BlockSpec(memory_space=pltpu.SMEM)
    stacked = 2 * G_C * QROWS
    if n_valid is None:
        grid = (KVW_C // LANES, B)
        kern = functools.partial(_attn_c_kernel, n_steps=Sq // QROWS)
        qspec = pl.BlockSpec((None, Sq, W_C // (KVW_C // LANES)), lambda p, b: (b, 0, p))
        kspec = pl.BlockSpec((None, Sq, LANES), lambda p, b: (b, 0, p))
        scratch = [pltpu.VMEM((2, stacked, C_WIN), F32), pltpu.VMEM((2, stacked, C_WIN), BF16),
                   pltpu.VMEM((2, stacked, LANES), F32)]
    else:
        assert Sq <= CHUNK and k.shape[2] == C_WIN
        grid = (B,)
        kern = functools.partial(_attn_c_window_kernel, n_valid=n_valid)
        qspec = pl.BlockSpec((None, Sq, W_C), lambda b: (b, 0, 0))
        kspec = pl.BlockSpec((None, KVW_C, C_WIN), lambda b: (b, 0, 0))
        scratch = []
    return pl.pallas_call(
        kern,
        grid=grid,
        in_specs=[sspec, qspec, kspec, kspec, qspec],
        out_specs=qspec,
        out_shape=jax.ShapeDtypeStruct(q.shape, BF16),
        scratch_shapes=scratch,
        compiler_params=_params(len(grid)),
        name="attn_c" if n_valid is None else "attn_c_window",
    )(sinks, q, k, v, sg)


def _c_head_order(w, axis):
    lead, trail = w.shape[:axis], w.shape[axis + 1:]
    w = w.reshape(*lead, KV_C // 2, 2, G_C, HEAD_DIM, *trail)
    return jnp.swapaxes(w, axis + 1, axis + 2).reshape(*lead, W_C, *trail)


def _heads(x2d, B, h):
    return x2d.reshape(B, x2d.shape[0] // B, h, HEAD_DIM)


def _flush(act):
    if not isinstance(act, tuple):
        return act
    merge, B, S, tm = act
    tm = MERGE_ROWS if (B * S) % MERGE_ROWS == 0 else tm
    return _merge(*merge, tm).reshape(B, S, D_MODEL)


def _ab_layer(x, cache, g_pre, w_in, w_out, g_post, table):
    x = _flush(x)
    B, S, _ = x.shape
    T = B * S
    x2d = x.reshape(T, D_MODEL)
    r3 = lambda t: t.reshape(B, S, -1)

    def cached(t, rows):
        g, w, c = t.shape
        t = t.reshape(g, w // HEAD_DIM, HEAD_DIM, c // rows, rows)
        return jnp.transpose(t, (0, 3, 4, 1, 2)).reshape(B, rows, w // HEAD_DIM, HEAD_DIM)

    if cache is None:
        tm = min(S, PROJ_ROWS)
        la = min(LEFT_CHUNKS_A * CHUNK, S)
        assert S % tm == 0 and la <= tm
        proj = _proj_ab(x2d, g_pre[None, :], w_in.astype(BF16), tm, S // tm, la)
        qa, ka, va, sga, qb, kb, vb, sgb = proj[:8]
        oa = _attn_a(r3(qa), r3(ka), r3(va), r3(sga), _bias_table(table, A_WIN), window=False)
        ob = _attn_b(r3(qb), r3(kb), r3(vb), r3(sgb))
    else:
        ca_k, ca_v, cb_k, cb_v = cache
        la = ca_k.shape[1]
        assert la == LEFT_CHUNKS_A * CHUNK and S <= CHUNK and cb_k.shape[1] % SB_BLK == 0
        keys_minor = lambda c: jnp.transpose(c, (0, 2, 3, 1)).reshape(B, c.shape[2] * HEAD_DIM, c.shape[1])
        tm = T
        proj = _proj_ab(x2d, g_pre[None, :], w_in.astype(BF16), tm, 1, tm)
        qa, ka, va, sga, qb, kb, vb, sgb = proj[:8]

        def window(c, new):
            parts = [keys_minor(c).astype(BF16), jnp.swapaxes(r3(new), 1, 2)]
            return jnp.pad(jnp.concatenate(parts, axis=2), ((0, 0), (0, 0), (0, A_WIN - la - S)))

        oa = _attn_a(r3(qa), window(ca_k, ka), window(ca_v, va), r3(sga), _bias_table(table, la + S), window=True)
        ob = _attn_b_cached(r3(qb), r3(kb), r3(vb), keys_minor(cb_k), keys_minor(cb_v), r3(sgb))
    rows = (cached(proj[8], min(la, S)), cached(proj[9], min(la, S)), cached(proj[10], S), cached(proj[11], S))
    merge = ([oa.reshape(T, W_A), ob.reshape(T, W_B)], w_out.astype(BF16), g_post[None, :], x2d)
    return (merge, B, S, tm), rows


def _c_layer(x, cache, past, g_pre, w_in, sinks, w_out, g_post):
    if isinstance(x, tuple):
        merge, B, S, _ = x
        x2d = None
    else:
        merge, (B, S, _) = None, x.shape
        x2d = x.reshape(B * S, D_MODEL)
    T = B * S
    g0 = W_C + 2 * KVW_C
    w_perm = jnp.concatenate([_c_head_order(w_in[:, :W_C], 1), w_in[:, W_C:g0], _c_head_order(w_in[:, g0:], 1)],
                             axis=1)
    r3 = lambda t: t.reshape(B, S, -1)
    if cache is None:
        tm = min(S, PROJ_ROWS)
        lc = min(WINDOW_C, S)
        assert S % tm == 0 and lc <= tm
        cos, sin = _rope_tables(S, 0, S)
        proj = _proj_c(x2d, g_pre[None, :], w_perm.astype(BF16), cos, sin, tm, S // tm, lc, merge)
        x2d, (q, k, v, sg, kf, vf) = (proj[0], proj[1:]) if merge is not None else (x2d, proj)
        o = _attn_c(r3(q), r3(k), r3(v), r3(sg), sinks)
    else:
        tm = T
        cos, sin = _rope_tables(T, past, S)
        proj = _proj_c(x2d, g_pre[None, :], w_perm.astype(BF16), cos, sin, tm, 1, tm, merge)
        x2d, (q, k, v, sg, kf, vf) = (proj[0], proj[1:]) if merge is not None else (x2d, proj)
        cc_k, cc_v = cache
        lc = cc_k.shape[1]
        assert lc == WINDOW_C and S <= CHUNK

        def window(c, new):
            old = jnp.transpose(c, (0, 2, 3, 1)).reshape(B, KVW_C, lc).astype(BF16)
            parts = [old, jnp.swapaxes(r3(new), 1, 2)]
            return jnp.pad(jnp.concatenate(parts, axis=2), ((0, 0), (0, 0), (0, C_WIN - lc - S)))

        o = _attn_c(r3(q), window(cc_k, k), window(cc_v, v), r3(sg), sinks, n_valid=lc + S)
    rows = (_heads(kf, B, KV_C), _heads(vf, B, KV_C))
    merge = ([o.reshape(T, W_C)], _c_head_order(w_out, 0).astype(BF16), g_post[None, :], x2d)
    return (merge, B, S, tm), rows


def kernel(x_prompt, x_sample, cache_a_k, cache_a_v, cache_b_k, cache_b_v, cache_c_k, cache_c_v,
           ab_norm_pre, ab_w_in, ab_w_out, ab_norm_post, a_rel_bias,
           c_norm_pre, c_w_in, c_sinks, c_w_out, c_norm_post):
    past = cache_b_k.shape[2]
    depth = ab_w_in.shape[0] + c_w_in.shape[0]
    yp, ys = x_prompt, x_sample
    ab_p, ab_s, c_p, c_s = [], [], [], []
    for layer in range(depth):
        i = layer // 2
        if layer % 2 == 0:
            w = (ab_norm_pre[i], ab_w_in[i], ab_w_out[i], ab_norm_post[i], a_rel_bias[i])
            yp, rows = _ab_layer(yp, None, *w)
            ab_p.append(rows)
            ys, rows = _ab_layer(ys, (cache_a_k[i], cache_a_v[i], cache_b_k[i], cache_b_v[i]), *w)
            ab_s.append(rows)
        else:
            w = (c_norm_pre[i], c_w_in[i], c_sinks[i], c_w_out[i], c_norm_post[i])
            yp, rows = _c_layer(yp, None, past, *w)
            c_p.append(rows)
            ys, rows = _c_layer(ys, (cache_c_k[i], cache_c_v[i]), past, *w)
            c_s.append(rows)
    st = lambda rows, j: jnp.stack([r[j] for r in rows])
    return (_flush(yp), _flush(ys),
            st(ab_p, 0), st(ab_p, 1), st(ab_p, 2), st(ab_p, 3), st(c_p, 0), st(c_p, 1),
            st(ab_s, 0), st(ab_s, 1), st(ab_s, 2), st(ab_s, 3), st(c_s, 0), st(c_s, 1))
```

```python
import functools

import jax
import jax.numpy as jnp
from jax import lax
from jax.experimental import pallas as pl
from jax.experimental.pallas import tpu as pltpu

D_MODEL = 1024
CHUNK = 64
HEAD_DIM = 64
H_A = 8
H_B = 8
LEFT_CHUNKS_A = 8
REL_CLIP_A = 128
N_REL_A = 2 * REL_CLIP_A + 1
W_A = H_A * HEAD_DIM
W_B = H_B * HEAD_DIM
H_C = 16
KV_C = 4
G_C = H_C // KV_C
WINDOW_C = 128
W_C = H_C * HEAD_DIM
KVW_C = KV_C * HEAD_DIM
ROPE_THETA = 10000.0
RMS_EPS = 1e-6
NEG_INF = -1e30
SCALE = HEAD_DIM ** -0.5

LANES = 128
QROWS = 128
A_WIN = (LEFT_CHUNKS_A + 2) * CHUNK
A_TAB = 384
A_ROLLW = A_WIN + QROWS
C_WIN = WINDOW_C + QROWS
MERGE_ROWS = 2048
STRIP = 32
LOG2E = 1.4426950408889634
SB_BLK = 256
SB_DEAD = 104.0 * LOG2E
VMEM_LIMIT = 56 * 1024 * 1024
PROJ_ROWS = 1024

F32 = jnp.float32
BF16 = jnp.bfloat16


def _params(n_axes, semantics="parallel", fuse_inputs=None):
    return pltpu.CompilerParams(dimension_semantics=(semantics,) * n_axes,
                                vmem_limit_bytes=VMEM_LIMIT, allow_input_fusion=fuse_inputs)


def _rms_scale(x):
    return lax.rsqrt(jnp.mean(x * x, axis=-1, keepdims=True) + RMS_EPS)


def _silu(u):
    return u / (1.0 + jnp.exp(-u))


def _low_half(shape):
    return lax.broadcasted_iota(jnp.int32, shape, 1) < HEAD_DIM


def _head_of_pair(x_bf16, low, h):
    xf = x_bf16.astype(F32)
    return jnp.where(low if h == 0 else jnp.logical_not(low), xf, 0.0).astype(BF16)


def _dot_nt(a, b):
    return lax.dot_general(a, b, (((1,), (1,)), ((), ())), preferred_element_type=F32)


def _proj_ab_kernel(x_ref, g_ref, w_ref, qa, ka, va, sga, qb, kb, vb, sgb, kaf, vaf, kbf, vbf,
                    *, tiles_per_seq, keep_a):
    tm = x_ref.shape[0]

    def body(with_tails):
        x = x_ref[...]
        xn = (x * _rms_scale(x) * g_ref[...]).astype(BF16)

        def col(j):
            return jnp.dot(xn, w_ref[:, j * W_A:(j + 1) * W_A], preferred_element_type=F32)

        sga[...] = _silu(col(3)).astype(BF16)
        sgb[...] = _silu(col(7)).astype(BF16)
        u = col(5); kb[...] = u.astype(BF16); kbf[...] = u.T
        u = col(6); vb[...] = u.astype(BF16); vbf[...] = u.T
        u = col(1); ka[...] = u.astype(BF16)
        if with_tails:
            kaf[...] = u[tm - keep_a:, :].T
        u = col(2); va[...] = u.astype(BF16)
        if with_tails:
            vaf[...] = u[tm - keep_a:, :].T
        qa[...] = (col(0) * (SCALE * LOG2E)).astype(BF16)
        qb[...] = (col(4) * (SCALE * LOG2E)).astype(BF16)

    if tiles_per_seq == 1:
        body(True)
    else:
        last = pl.program_id(0) % tiles_per_seq == tiles_per_seq - 1
        pl.when(last)(functools.partial(body, True))
        pl.when(jnp.logical_not(last))(functools.partial(body, False))


def _proj_ab(x2d, g_pre, w_bf16, tm, tiles_per_seq, keep_a):
    T = x2d.shape[0]
    n_tiles = T // tm
    row = lambda w: pl.BlockSpec((tm, w), lambda i: (i, 0))
    const = lambda a: pl.BlockSpec(a.shape, lambda i: (0, 0), pipeline_mode=pl.Buffered(1))
    bshape = jax.ShapeDtypeStruct((T, W_A), BF16)
    groups = n_tiles // tiles_per_seq
    a_spec = pl.BlockSpec((None, W_A, keep_a), lambda i: (i // tiles_per_seq, 0, 0))
    a_shape = jax.ShapeDtypeStruct((groups, W_A, keep_a), F32)
    b_spec = pl.BlockSpec((None, W_B, tm), lambda i: (i // tiles_per_seq, 0, i % tiles_per_seq))
    b_shape = jax.ShapeDtypeStruct((groups, W_B, tiles_per_seq * tm), F32)
    return pl.pallas_call(
        functools.partial(_proj_ab_kernel, tiles_per_seq=tiles_per_seq, keep_a=keep_a),
        grid=(n_tiles,),
        in_specs=[row(D_MODEL), const(g_pre), const(w_bf16)],
        out_specs=[row(W_A)] * 8 + [a_spec, a_spec, b_spec, b_spec],
        out_shape=[bshape] * 8 + [a_shape, a_shape, b_shape, b_shape],
        compiler_params=_params(1, "arbitrary", [False, False, True]),
        name="proj_ab",
    )(x2d, g_pre, w_bf16)


def _merge_body(h_refs, w_ref, g_ref, x_ref, y_ref):
    acc = None
    off = 0
    for h in h_refs:
        k = h.shape[1]
        d = jnp.dot(h[...], w_ref[off:off + k, :], preferred_element_type=F32)
        acc = d if acc is None else acc + d
        off += k
    y = x_ref[...] + acc * _rms_scale(acc) * g_ref[...]
    y_ref[...] = y
    return y


def _merge_kernel(*refs, n_parts):
    _merge_body(refs[:n_parts], *refs[n_parts:])


def _merge(parts, w_bf16, g_post, x2d, tm):
    T = x2d.shape[0]
    row = lambda w: pl.BlockSpec((tm, w), lambda i: (i, 0))
    full = lambda a: pl.BlockSpec(a.shape, lambda i: (0, 0))
    return pl.pallas_call(
        functools.partial(_merge_kernel, n_parts=len(parts)),
        grid=(T // tm,),
        in_specs=[row(p.shape[1]) for p in parts] + [full(w_bf16), full(g_post), row(D_MODEL)],
        out_specs=row(D_MODEL),
        out_shape=jax.ShapeDtypeStruct((T, D_MODEL), F32),
        compiler_params=_params(1, fuse_inputs=[False] * len(parts) + [True, False, False]),
        name="merge",
    )(*parts, w_bf16, g_post, x2d)


def _bias_kernel(tab_ref, out_ref, *, n_valid):
    m = lax.broadcasted_iota(jnp.int32, (A_TAB, A_ROLLW), 1)
    t = lax.broadcasted_iota(jnp.int32, (A_TAB, A_ROLLW), 0)
    idx = jnp.where(m <= A_WIN, jnp.clip(A_WIN - QROWS - m, -REL_CLIP_A, REL_CLIP_A) + REL_CLIP_A,
                    2 * REL_CLIP_A)
    onehot = jnp.where(t == idx, 1.0, 0.0).astype(BF16)
    tab = tab_ref[...]
    hi = tab.astype(BF16)
    r1 = tab - hi.astype(F32)
    mid = r1.astype(BF16)
    lo = (r1 - mid.astype(F32)).astype(BF16)
    src = (jnp.dot(hi, onehot, preferred_element_type=F32)
           + jnp.dot(mid, onehot, preferred_element_type=F32)
           + jnp.dot(lo, onehot, preferred_element_type=F32))
    i = lax.broadcasted_iota(jnp.int32, (QROWS, A_WIN), 0)
    j = lax.broadcasted_iota(jnp.int32, (QROWS, A_WIN), 1)
    qc = LEFT_CHUNKS_A + i // CHUNK
    kc = j // CHUNK
    valid = (kc <= qc) & (kc >= qc - LEFT_CHUNKS_A) & (j < n_valid)
    for h in range(H_A):
        row = jnp.broadcast_to(src[h:h + 1, :], (QROWS, A_ROLLW))
        toep = pltpu.roll(row, 0, 1, stride=1, stride_axis=0)
        out_ref[h * QROWS:(h + 1) * QROWS, :] = jnp.where(valid, toep[:, :A_WIN] * LOG2E, NEG_INF)


def _bias_table(table, n_valid):
    tab = jnp.pad(table, ((0, 0), (0, A_TAB - N_REL_A)))
    return pl.pallas_call(
        functools.partial(_bias_kernel, n_valid=n_valid),
        out_shape=jax.ShapeDtypeStruct((H_A * QROWS, A_WIN), F32),
        name="a_bias",
    )(tab)


def _pipeline(n_units, n_static, period, stages):
    depth = len(stages)

    def tick(t, tmod):
        for lag in reversed(range(depth)):
            u = t - lag
            if not isinstance(u, int) or 0 <= u < n_units:
                stages[lag](u, (tmod - lag) % period)

    first = n_static + depth - 1
    for t in range(min(first, n_units + depth - 1)):
        tick(t, t % period)
    groups = max(n_units - first, 0) // period

    def body(g, c):
        t = first + period * g
        for d in range(period):
            tick(t + d, (first + d) % period)
        return c

    if groups:
        lax.fori_loop(0, groups, body, 0)
    for t in range(max(first, first + period * groups), n_units + depth - 1):
        tick(t, t % period)


def _softmax_strips(s_scr, p_scr, width):
    for r in range(0, s_scr.shape[0], STRIP):
        x = s_scr[r:r + STRIP, :width]
        p_scr[r:r + STRIP, :width] = jnp.exp2(x - jnp.max(x, axis=-1, keepdims=True)).astype(BF16)


def _attn_a_stages(q_ref, k_ref, v_ref, sg_ref, bias_ref, o_ref, s_scr, p_scr, geom):
    low = _low_half((QROWS, LANES))

    def qk(m, slot):
        q0, k0, width, bcol0 = geom(m)
        qm = q_ref[pl.ds(q0, QROWS), :]
        q2 = jnp.concatenate([_head_of_pair(qm, low, 0), _head_of_pair(qm, low, 1)], axis=0)
        s_scr[slot, :, :width] = _dot_nt(q2, k_ref[pl.ds(k0, width), :]) + bias_ref[:, bcol0:bcol0 + width]

    def sm(m, slot):
        _softmax_strips(s_scr.at[slot], p_scr.at[slot], geom(m)[2])

    def pv(m, slot):
        q0, k0, width, _ = geom(m)
        vw = v_ref[pl.ds(k0, width), :]
        o2 = jnp.dot(p_scr[slot, :, :width], jnp.concatenate([vw, jnp.ones_like(vw)], axis=1),
                     preferred_element_type=F32)
        top, bot = o2[:QROWS], o2[QROWS:]
        o = jnp.where(low, top[:, :LANES] / top[:, LANES:], bot[:, :LANES] / bot[:, LANES:])
        o_ref[pl.ds(q0, QROWS), :] = (o * sg_ref[pl.ds(q0, QROWS), :].astype(F32)).astype(BF16)

    return qk, sm, pv


def _attn_a_kernel(q_ref, k_ref, v_ref, sg_ref, bias_ref, o_ref, s_scr, p_scr, *, n_steps):
    lead = A_WIN // QROWS - 1

    def geom(m):
        if isinstance(m, int) and m < lead:
            return m * QROWS, 0, (m + 1) * QROWS, (lead - m) * QROWS
        if isinstance(m, int):
            return m * QROWS, (m - lead) * QROWS, A_WIN, 0
        return pl.multiple_of(m * QROWS, QROWS), pl.multiple_of((m - lead) * QROWS, QROWS), A_WIN, 0

    _pipeline(n_steps, lead, 2, _attn_a_stages(q_ref, k_ref, v_ref, sg_ref, bias_ref, o_ref, s_scr, p_scr, geom))


def _attn_a_window_kernel(q_ref, kt_ref, vt_ref, sg_ref, bias_ref, o_ref, s_scr, p_scr):
    S = q_ref.shape[0]
    low = _low_half((S, LANES))
    for p in range(W_A // LANES):
        cols = slice(p * LANES, (p + 1) * LANES)
        qm = q_ref[:, cols]
        q2 = jnp.concatenate([_head_of_pair(qm, low, 0), _head_of_pair(qm, low, 1)], axis=0)
        b0 = 2 * p * QROWS
        bias = jnp.concatenate([bias_ref[b0:b0 + S, :], bias_ref[b0 + QROWS:b0 + QROWS + S, :]], axis=0)
        s_view, p_view = s_scr.at[p % 2, pl.ds(0, 2 * S)], p_scr.at[p % 2, pl.ds(0, 2 * S)]
        s_view[...] = jnp.dot(q2, kt_ref[cols, :], preferred_element_type=F32) + bias
        _softmax_strips(s_view, p_view, A_WIN)
        vt = vt_ref[cols, :]
        o2 = _dot_nt(p_view[...], jnp.concatenate([vt, jnp.ones_like(vt)], axis=0))
        o = jnp.where(low, o2[:S, :LANES] / o2[:S, LANES:], o2[S:, :LANES] / o2[S:, LANES:])
        o_ref[:, cols] = (o * sg_ref[:, cols].astype(F32)).astype(BF16)


def _attn_a(q, k, v, sg, bias, window):
    B, Sq, _ = q.shape
    if window:
        assert Sq % STRIP == 0 and Sq <= CHUNK and k.shape[2] == A_WIN
        grid = (B,)
        qspec = pl.BlockSpec((None, Sq, W_A), lambda b: (b, 0, 0))
        kspec = pl.BlockSpec((None, W_A, A_WIN), lambda b: (b, 0, 0))
        bspec = pl.BlockSpec(bias.shape, lambda b: (0, 0))
        kern = _attn_a_window_kernel
    else:
        grid = (W_A // LANES, B)
        qspec = kspec = pl.BlockSpec((None, Sq, LANES), lambda p, b: (b, 0, p))
        bspec = pl.BlockSpec((2 * QROWS, A_WIN), lambda p, b: (p, 0))
        kern = functools.partial(_attn_a_kernel, n_steps=Sq // QROWS)
    return pl.pallas_call(
        kern,
        grid=grid,
        in_specs=[qspec, kspec, kspec, qspec, bspec],
        out_specs=qspec,
        out_shape=jax.ShapeDtypeStruct(q.shape, BF16),
        scratch_shapes=[pltpu.VMEM((2, 2 * QROWS, A_WIN), F32), pltpu.VMEM((2, 2 * QROWS, A_WIN), BF16)],
        compiler_params=_params(len(grid)),
        name="attn_a_window" if window else "attn_a",
    )(q, k, v, sg, bias)


def _sb_stages(scr, unit, past_transposed=False, qrows=SB_BLK, diag_w=SB_BLK):
    z_scr, h_scr, a_scr, rs_scr = scr
    rows = 2 * qrows
    low = _low_half((qrows, LANES))
    before = {w: lax.broadcasted_iota(jnp.int32, (STRIP, w), 1) - lax.broadcasted_iota(jnp.int32, (STRIP, w), 0)
              for w in {LANES, SB_BLK, diag_w}}
    tri = {}
    for kw in {SB_BLK, diag_w}:
        r = lax.broadcasted_iota(jnp.int32, (kw, kw), 0)
        c = lax.broadcasted_iota(jnp.int32, (kw, kw), 1)
        tri[kw] = jnp.where(r > c, 1.0, 0.0).astype(BF16)
    keys = lambda diag: diag_w if diag else SB_BLK

    def scores(u, us):
        q_ref, q0, k_ref, _, k0, diag = unit(u, us)[:6]
        qm = q_ref[pl.ds(q0, qrows), :]
        q2 = jnp.concatenate([_head_of_pair(qm, low, 0), _head_of_pair(qm, low, 1)], axis=0)
        kw = keys(diag)
        if past_transposed and not diag:
            z_scr[us % 4, :, :kw] = jnp.dot(q2, k_ref[:, pl.ds(k0, kw)].astype(BF16), preferred_element_type=F32)
        else:
            z_scr[us % 4, :, :kw] = _dot_nt(q2, k_ref[pl.ds(k0, kw), :].astype(BF16))

    def width(diag, r0):
        return LANES if diag and r0 % qrows + STRIP <= LANES else keys(diag)

    def terms(u, us):
        diag = unit(u, us)[5]
        zs, hs, rs = z_scr.at[us % 4], h_scr.at[us % 2], rs_scr.at[us % 2]
        for r0 in range(0, rows, STRIP):
            w = width(diag, r0)
            z = zs[r0:r0 + STRIP, :w]
            sp = jnp.maximum(z, 0.0) + jnp.log2(1.0 + jnp.exp2(-jnp.abs(z)))
            zs[r0:r0 + STRIP, :w] = z - sp
            if diag:
                sp = jnp.where(before[w] < r0 % qrows, sp, 0.0)
            hs[r0:r0 + STRIP, :w] = sp.astype(BF16)
            if w < keys(diag):
                hs[r0:r0 + STRIP, w:keys(diag)] = jnp.zeros((STRIP, keys(diag) - w), BF16)
            rs[r0:r0 + STRIP, :] = jnp.broadcast_to(jnp.sum(sp, axis=-1, keepdims=True), (STRIP, LANES))

    def weights(u, us):
        diag, dec = unit(u, us)[5:7]
        zs, rs, a_s = z_scr.at[us % 4], rs_scr.at[us % 2], a_scr.at[us % 2]
        kw = keys(diag)
        inner = jnp.dot(h_scr[us % 2, :, :kw], tri[kw], preferred_element_type=F32)
        for r0 in range(0, rows, STRIP):
            w = width(diag, r0)
            x = zs[r0:r0 + STRIP, :w] - inner[r0:r0 + STRIP, :w]
            if diag:
                dec[r0:r0 + STRIP, :] = rs[r0:r0 + STRIP, :]
            else:
                d = dec[r0:r0 + STRIP, :]
                x = x - jnp.concatenate([d, d], axis=1)
                dec[r0:r0 + STRIP, :] = d + rs[r0:r0 + STRIP, :]
            a = jnp.exp2(x)
            if diag:
                a = jnp.where(before[w] < r0 % qrows, a, 0.0)
            a_s[r0:r0 + STRIP, :w] = a.astype(BF16)
            if w < kw:
                a_s[r0:r0 + STRIP, w:kw] = jnp.zeros((STRIP, kw - w), BF16)

    def values(u, us):
        _, _, _, v_ref, k0, diag, _, acc = unit(u, us)[:8]
        kw = keys(diag)
        if past_transposed and not diag:
            pv = _dot_nt(a_scr[us % 2, :, :kw], v_ref[:, pl.ds(k0, kw)].astype(BF16))
        else:
            pv = jnp.dot(a_scr[us % 2, :, :kw], v_ref[pl.ds(k0, kw), :].astype(BF16),
                         preferred_element_type=F32)
        if diag:
            acc[...] = pv
        else:
            acc[...] += pv
        if len(unit(u, us)) > 8:
            unit(u, us)[8]()

    return scores, terms, weights, values


def _sb_live(dec):
    return (jnp.min(dec[...]) < SB_DEAD).astype(jnp.int32)


def _sb_deeper(stages, j0, dec, fetch=None):
    def body(c):
        if fetch is not None:
            fetch(c[0])
        for stage in stages:
            stage(c[0], 0)
        return c[0] - 1, _sb_live(dec)

    lax.while_loop(lambda c: (c[0] >= 0) & (c[1] > 0), body, (jnp.asarray(j0, jnp.int32), jnp.int32(1)))


def _sb_scratch(n_qblocks, qrows=SB_BLK):
    rows = 2 * qrows
    return [pltpu.VMEM((4, rows, SB_BLK), F32), pltpu.VMEM((2, rows, SB_BLK), BF16),
            pltpu.VMEM((2, rows, SB_BLK), BF16), pltpu.VMEM((2, rows, LANES), F32),
            pltpu.VMEM((n_qblocks, rows, LANES), F32), pltpu.VMEM((n_qblocks, rows, LANES), F32),
            pltpu.SMEM((n_qblocks,), jnp.int32)]


def _sb_output(acc, sg):
    qrows = sg.shape[0]
    low = _low_half((qrows, LANES))
    return (jnp.where(low, acc[:qrows, :], acc[qrows:, :]) * sg.astype(F32)).astype(BF16)


def _attn_b_kernel(q_ref, k_ref, v_ref, sg_ref, o_ref, z_scr, h_scr, a_scr, rs_scr, dec_all, acc_all, live):
    n_q = q_ref.shape[0] // SB_BLK
    scr = (z_scr, h_scr, a_scr, rs_scr)
    blk = lambda i: i * SB_BLK if isinstance(i, int) else pl.multiple_of(i * SB_BLK, SB_BLK)

    def emit(i):
        rows = pl.ds(blk(i), SB_BLK)
        o_ref[rows, :] = _sb_output(acc_all[i], sg_ref[rows, :])

    def unit(u, us):
        i = (u + 1) // 2
        diag = u % 2 == 1 or u == 0 if isinstance(u, int) else us % 2 == 1
        base = (q_ref, blk(i), k_ref, v_ref, blk(i if diag else i - 1), diag, dec_all.at[i], acc_all.at[i])
        if diag and not (isinstance(u, int) and u == 0):
            return base

        def done():
            emit(i)
            live[i] = _sb_live(dec_all.at[i])

        return base + (done,)

    _pipeline(2 * n_q - 1, 1, 4, _sb_stages(scr, unit))

    def finish(i, c):
        @pl.when(live[i] > 0)
        def _():
            unit_at = lambda j, us: (q_ref, blk(i), k_ref, v_ref, blk(j), False, dec_all.at[i], acc_all.at[i])
            _sb_deeper(_sb_stages(scr, unit_at), i - 2, dec_all.at[i])
            emit(i)

        return c

    lax.fori_loop(2, n_q, finish, 0)


def _attn_b_cached_kernel(q_ref, kd_ref, vd_ref, kl_ref, vl_ref, kp_hbm, vp_hbm, sg_ref, o_ref,
                          z_scr, h_scr, a_scr, rs_scr, dec_all, acc_all, live, kd_pad, vd_pad, kbuf, vbuf, sem):
    n_past = kp_hbm.shape[2] // SB_BLK
    n_pairs = W_B // LANES
    batch = pl.program_id(0)
    scr = (z_scr, h_scr, a_scr, rs_scr)
    qrows = q_ref.shape[0]
    shape = dict(past_transposed=True, qrows=qrows, diag_w=LANES)
    lanes = lambda p: pl.ds(p * LANES, LANES)
    state = lambda p: (dec_all.at[p], acc_all.at[p])
    for pad, src in ((kd_pad, kd_ref), (vd_pad, vd_ref)):
        pad[...] = jnp.zeros(pad.shape, BF16)
        for p in range(n_pairs):
            pad[p, :qrows, :] = src[:, p * LANES:(p + 1) * LANES]

    def emit(p):
        o_ref[:, p * LANES:(p + 1) * LANES] = _sb_output(acc_all[p], sg_ref[:, p * LANES:(p + 1) * LANES])

    def unit(u, us):
        p = u // 2
        q = q_ref.at[:, lanes(p)]
        if u % 2 == 0:
            return (q, 0, kd_pad.at[p], vd_pad.at[p], 0, True) + state(p)

        def done():
            emit(p)
            live[p] = _sb_live(dec_all.at[p])

        return (q, 0, kl_ref.at[lanes(p), :], vl_ref.at[lanes(p), :], 0, False) + state(p) + (done,)

    _pipeline(2 * n_pairs, 2 * n_pairs, 4, _sb_stages(scr, unit, **shape))

    for p in range(n_pairs):
        @pl.when(live[p] > 0)
        def _(p=p):
            def fetch(j):
                cols = pl.ds(pl.multiple_of(j * SB_BLK, SB_BLK), SB_BLK)
                copies = [pltpu.make_async_copy(hbm.at[batch, lanes(p), cols], buf, sem.at[n])
                          for n, (hbm, buf) in enumerate(((kp_hbm, kbuf), (vp_hbm, vbuf)))]
                for c in copies:
                    c.start()
                for c in copies:
                    c.wait()

            unit_at = lambda j, us: (q_ref.at[:, lanes(p)], 0, kbuf, vbuf, 0, False) + state(p)
            _sb_deeper(_sb_stages(scr, unit_at, **shape), n_past - 2, dec_all.at[p], fetch)
            emit(p)


def _attn_b(q, k, v, sg):
    B, S, _ = q.shape
    spec = pl.BlockSpec((None, S, LANES), lambda p, b: (b, 0, p))
    return pl.pallas_call(
        _attn_b_kernel,
        grid=(W_B // LANES, B),
        in_specs=[spec] * 4,
        out_specs=spec,
        out_shape=jax.ShapeDtypeStruct(q.shape, BF16),
        scratch_shapes=_sb_scratch(S // SB_BLK),
        compiler_params=_params(2),
        name="attn_b",
    )(q, k, v, sg)


def _attn_b_cached(q, kd, vd, kp, vp, sg):
    B, S, _ = q.shape
    P = kp.shape[2]
    assert S % STRIP == 0 and S <= LANES and P % SB_BLK == 0
    n_pairs = W_B // LANES
    spec = pl.BlockSpec((None, S, W_B), lambda b: (b, 0, 0))
    last = pl.BlockSpec((None, W_B, SB_BLK), lambda b: (b, 0, P // SB_BLK - 1))
    hbm = pl.BlockSpec(memory_space=pl.ANY)
    return pl.pallas_call(
        _attn_b_cached_kernel,
        grid=(B,),
        in_specs=[spec, spec, spec, last, last, hbm, hbm, spec],
        out_specs=spec,
        out_shape=jax.ShapeDtypeStruct(q.shape, BF16),
        scratch_shapes=(_sb_scratch(n_pairs, S) + [pltpu.VMEM((n_pairs, LANES, LANES), BF16)] * 2
                        + [pltpu.VMEM((LANES, SB_BLK), F32)] * 2 + [pltpu.SemaphoreType.DMA((2,))]),
        compiler_params=_params(1),
        name="attn_b_cached",
    )(q, kd, vd, kp, vp, kp, vp, sg)


def _rope_table_kernel(inv_ref, cos_ref, sin_ref, *, pos0, period):
    shape = cos_ref.shape
    row = lax.broadcasted_iota(jnp.int32, shape, 0)
    lane = lax.broadcasted_iota(jnp.int32, shape, 1)
    ang = (pos0 + row % period).astype(F32) * inv_ref[...]
    cos_ref[...] = jnp.cos(ang)
    s = jnp.sin(ang)
    sin_ref[...] = jnp.where(lane % HEAD_DIM < HEAD_DIM // 2, -s, s)


def _rope_tables(n_rows, pos0, period):
    half = HEAD_DIM // 2
    inv = ROPE_THETA ** (-jnp.arange(half, dtype=F32) * (2.0 / HEAD_DIM))
    inv = jnp.tile(inv, LANES // half)[None, :]
    shape = jax.ShapeDtypeStruct((n_rows, LANES), F32)
    return pl.pallas_call(
        functools.partial(_rope_table_kernel, pos0=pos0, period=period),
        out_shape=[shape, shape],
        name="rope_table",
    )(inv)


def _proj_c_kernel(*refs, n_parts):
    if n_parts:
        w_out, g_post, x_ref, g_ref, w_ref, cos_ref, sin_ref, y_ref = refs[n_parts:n_parts + 8]
        x = _merge_body(refs[:n_parts], w_out, g_post, x_ref, y_ref)
        _proj_c_body(x, g_ref, w_ref, cos_ref, sin_ref, *refs[n_parts + 8:])
    else:
        _proj_c_body(refs[0][...], *refs[1:])


def _proj_c_body(x, g_ref, w_ref, cos_ref, sin_ref, q_o, k_o, v_o, sg_o, kf_o, vf_o):
    xn = (x * _rms_scale(x) * g_ref[...]).astype(BF16)
    tm = x.shape[0]
    keep = kf_o.shape[0]

    def keep_tail(o_ref, u):
        o_ref[...] = u[tm - keep:, :]

    cos = cos_ref[...]
    sin = sin_ref[...]
    first = lax.broadcasted_iota(jnp.int32, cos.shape, 1) % HEAD_DIM < HEAD_DIM // 2

    def rope(u):
        partner = jnp.where(first, pltpu.roll(u, LANES - HEAD_DIM // 2, 1), pltpu.roll(u, HEAD_DIM // 2, 1))
        return u * cos + partner * sin

    blk = 4 * LANES
    for c0 in range(0, W_C, blk):
        u = jnp.dot(xn, w_ref[:, c0:c0 + blk], preferred_element_type=F32)
        for j in range(0, blk, LANES):
            q_o[:, c0 + j:c0 + j + LANES] = (rope(u[:, j:j + LANES]) * (SCALE * LOG2E)).astype(BF16)
    u = jnp.dot(xn, w_ref[:, W_C:W_C + KVW_C], preferred_element_type=F32)
    r = jnp.concatenate([rope(u[:, j:j + LANES]) for j in range(0, KVW_C, LANES)], axis=1)
    k_o[...] = r.astype(BF16)
    keep_tail(kf_o, r)
    u = jnp.dot(xn, w_ref[:, W_C + KVW_C:W_C + 2 * KVW_C], preferred_element_type=F32)
    v_o[...] = u.astype(BF16)
    keep_tail(vf_o, u)
    g0 = W_C + 2 * KVW_C
    for c0 in range(0, W_C, blk):
        u = jnp.dot(xn, w_ref[:, g0 + c0:g0 + c0 + blk], preferred_element_type=F32)
        sg_o[:, c0:c0 + blk] = _silu(u).astype(BF16)


def _proj_c(x2d, g_pre, w_bf16, cos, sin, tm, tiles_per_seq, keep, merge=None):
    parts, w_out, g_post, x_res = merge if merge is not None else ((), None, None, None)
    T = (x_res if merge is not None else x2d).shape[0]
    n_tiles = T // tm
    nt = cos.shape[0] // tm
    row = lambda w: pl.BlockSpec((tm, w), lambda i: (i, 0))
    const = lambda a: pl.BlockSpec(a.shape, lambda i: (0, 0), pipeline_mode=pl.Buffered(1))
    tab = pl.BlockSpec((tm, LANES), lambda i: (i % nt, 0))
    tail = pl.BlockSpec((keep, KVW_C), lambda i: (i // tiles_per_seq, 0))
    sd = jax.ShapeDtypeStruct
    tail_shape = sd((n_tiles // tiles_per_seq * keep, KVW_C), F32)
    if merge is not None:
        operands = [*parts, w_out, g_post, x_res]
        in_specs = [row(p.shape[1]) for p in parts] + [const(w_out), const(g_post), row(D_MODEL)]
        out_specs, out_shape = [row(D_MODEL)], [sd((T, D_MODEL), F32)]
    else:
        operands, in_specs, out_specs, out_shape = [x2d], [row(D_MODEL)], [], []
    return pl.pallas_call(
        functools.partial(_proj_c_kernel, n_parts=len(parts)),
        grid=(n_tiles,),
        in_specs=in_specs + [const(g_pre), const(w_bf16), tab, tab],
        out_specs=out_specs + [row(W_C), row(KVW_C), row(KVW_C), row(W_C), tail, tail],
        out_shape=out_shape + [sd((T, W_C), BF16), sd((T, KVW_C), BF16), sd((T, KVW_C), BF16),
                               sd((T, W_C), BF16), tail_shape, tail_shape],
        compiler_params=_params(1, "arbitrary", [op is w_out for op in operands] + [False, True, False, False]),
        name="merge_proj_c" if merge is not None else "proj_c",
    )(*operands, g_pre, w_bf16, cos, sin)


def _attn_c_stages(sink_ref, q_ref, k_ref, v_ref, sg_ref, o_ref, s_scr, p_scr, e_scr, geom, n_valid):
    low = _low_half((QROWS, LANES))
    blocks = [(hh, g) for hh in range(2) for g in range(G_C)]

    def qk(m, slot):
        q0, k0, width, mcol0 = geom(m)
        i = lax.broadcasted_iota(jnp.int32, (QROWS, width), 0) // CHUNK
        j = lax.broadcasted_iota(jnp.int32, (QROWS, width), 1)
        jc = (j + mcol0) // CHUNK
        valid = (jc <= i + WINDOW_C // CHUNK) & (jc >= i)
        if n_valid is not None:
            valid = valid & (j < n_valid)
        mask = jnp.where(valid, 0.0, NEG_INF)
        rows = pl.ds(q0, QROWS)
        q2 = jnp.concatenate([_head_of_pair(q_ref[rows, g * LANES:(g + 1) * LANES], low, hh)
                              for hh, g in blocks], axis=0)
        z = _dot_nt(q2, k_ref[pl.ds(k0, width), :])
        for bi in range(len(blocks)):
            s_scr[slot, bi * QROWS:(bi + 1) * QROWS, :width] = z[bi * QROWS:(bi + 1) * QROWS] + mask

    def sm(m, slot):
        width = geom(m)[2]
        pair = pl.program_id(0)
        for bi, (hh, g) in enumerate(blocks):
            sink = sink_ref[(2 * pair + hh) * G_C + g] * LOG2E
            for r in range(bi * QROWS, (bi + 1) * QROWS, STRIP):
                x = s_scr[slot, r:r + STRIP, :width]
                mx = jnp.maximum(jnp.max(x, axis=-1, keepdims=True), sink)
                p_scr[slot, r:r + STRIP, :width] = jnp.exp2(x - mx).astype(BF16)
                e_scr[slot, r:r + STRIP, :] = jnp.broadcast_to(jnp.exp2(sink - mx), (STRIP, LANES))

    def pv(m, slot):
        q0, k0, width, _ = geom(m)
        rows = pl.ds(q0, QROWS)
        vw = v_ref[pl.ds(k0, width), :]
        o2 = jnp.dot(p_scr[slot, :, :width], jnp.concatenate([vw, jnp.ones_like(vw)], axis=1),
                     preferred_element_type=F32)
        for g in range(G_C):
            t0, b0 = g * QROWS, (G_C + g) * QROWS
            top = o2[t0:t0 + QROWS, :LANES] / (o2[t0:t0 + QROWS, LANES:] + e_scr[slot, t0:t0 + QROWS, :])
            bot = o2[b0:b0 + QROWS, :LANES] / (o2[b0:b0 + QROWS, LANES:] + e_scr[slot, b0:b0 + QROWS, :])
            cols = slice(g * LANES, (g + 1) * LANES)
            o_ref[rows, cols] = (jnp.where(low, top, bot) * sg_ref[rows, cols].astype(F32)).astype(BF16)

    return qk, sm, pv


def _attn_c_kernel(sink_ref, q_ref, k_ref, v_ref, sg_ref, o_ref, s_scr, p_scr, e_scr, *, n_steps):
    def geom(m):
        if isinstance(m, int) and m == 0:
            return 0, 0, QROWS, C_WIN - QROWS
        if isinstance(m, int):
            return m * QROWS, (m - 1) * QROWS, C_WIN, 0
        return pl.multiple_of(m * QROWS, QROWS), pl.multiple_of((m - 1) * QROWS, QROWS), C_WIN, 0

    _pipeline(n_steps, 1, 2, _attn_c_stages(sink_ref, q_ref, k_ref, v_ref, sg_ref, o_ref,
                                            s_scr, p_scr, e_scr, geom, None))


def _attn_c_window_kernel(sink_ref, q_ref, kt_ref, vt_ref, sg_ref, o_ref, *, n_valid):
    for pair in range(KVW_C // LANES):
        qw = W_C // (KVW_C // LANES)
        kv = slice(pair * LANES, (pair + 1) * LANES)
        _attn_c_window_pair(sink_ref, q_ref.at[:, pl.ds(pair * qw, qw)], kt_ref.at[kv, :], vt_ref.at[kv, :],
                            sg_ref.at[:, pl.ds(pair * qw, qw)], o_ref.at[:, pl.ds(pair * qw, qw)], pair, n_valid)


def _attn_c_window_pair(sink_ref, q_ref, kt_ref, vt_ref, sg_ref, o_ref, pair, n_valid):
    S = q_ref.shape[0]
    low = _low_half((S, LANES))
    j = lax.broadcasted_iota(jnp.int32, (S, C_WIN), 1)
    mask = jnp.where((j // CHUNK <= WINDOW_C // CHUNK) & (j < n_valid), 0.0, NEG_INF)
    blocks = [(hh, g) for hh in range(2) for g in range(G_C)]
    q2 = jnp.concatenate([_head_of_pair(q_ref[:, g * LANES:(g + 1) * LANES], low, hh) for hh, g in blocks], axis=0)
    z = jnp.dot(q2, kt_ref[...], preferred_element_type=F32)
    probs, sink_terms = [], []
    for bi, (hh, g) in enumerate(blocks):
        x = z[bi * S:(bi + 1) * S] + mask
        sink = sink_ref[(2 * pair + hh) * G_C + g] * LOG2E
        mx = jnp.maximum(jnp.max(x, axis=-1, keepdims=True), sink)
        probs.append(jnp.exp2(x - mx).astype(BF16))
        sink_terms.append(jnp.exp2(sink - mx))
    vt = vt_ref[...]
    o2 = _dot_nt(jnp.concatenate(probs, axis=0), jnp.concatenate([vt, jnp.ones_like(vt)], axis=0))
    for g in range(G_C):
        halves = []
        for bi in (g, G_C + g):
            blk = o2[bi * S:(bi + 1) * S]
            halves.append(blk[:, :LANES] / (blk[:, LANES:] + sink_terms[bi]))
        cols = slice(g * LANES, (g + 1) * LANES)
        o_ref[:, cols] = (jnp.where(low, *halves) * sg_ref[:, cols].astype(F32)).astype(BF16)


def _attn_c(q, k, v, sg, sinks, n_valid=None):
    B, Sq, _ = q.shape
    sspec = pl.BlockSpec(memory_space=pltpu.SMEM)
    stacked = 2 * G_C * QROWS
    if n_valid is None:
        grid = (KVW_C // LANES, B)
        kern = functools.partial(_attn_c_kernel, n_steps=Sq // QROWS)
        qspec = pl.BlockSpec((None, Sq, W_C // (KVW_C // LANES)), lambda p, b: (b, 0, p))
        kspec = pl.BlockSpec((None, Sq, LANES), lambda p, b: (b, 0, p))
        scratch = [pltpu.VMEM((2, stacked, C_WIN), F32), pltpu.VMEM((2, stacked, C_WIN), BF16),
                   pltpu.VMEM((2, stacked, LANES), F32)]
    else:
        assert Sq <= CHUNK and k.shape[2] == C_WIN
        grid = (B,)
        kern = functools.partial(_attn_c_window_kernel, n_valid=n_valid)
        qspec = pl.BlockSpec((None, Sq, W_C), lambda b: (b, 0, 0))
        kspec = pl.BlockSpec((None, KVW_C, C_WIN), lambda b: (b, 0, 0))
        scratch = []
    return pl.pallas_call(
        kern,
        grid=grid,
        in_specs=[sspec, qspec, kspec, kspec, qspec],
        out_specs=qspec,
        out_shape=jax.ShapeDtypeStruct(q.shape, BF16),
        scratch_shapes=scratch,
        compiler_params=_params(len(grid)),
        name="attn_c" if n_valid is None else "attn_c_window",
    )(sinks, q, k, v, sg)


def _c_head_order(w, axis):
    lead, trail = w.shape[:axis], w.shape[axis + 1:]
    w = w.reshape(*lead, KV_C // 2, 2, G_C, HEAD_DIM, *trail)
    return jnp.swapaxes(w, axis + 1, axis + 2).reshape(*lead, W_C, *trail)


def _heads(x2d, B, h):
    return x2d.reshape(B, x2d.shape[0] // B, h, HEAD_DIM)


def _flush(act):
    if not isinstance(act, tuple):
        return act
    merge, B, S, tm = act
    tm = MERGE_ROWS if (B * S) % MERGE_ROWS == 0 else tm
    return _merge(*merge, tm).reshape(B, S, D_MODEL)


def _ab_layer(x, cache, g_pre, w_in, w_out, g_post, table):
    x = _flush(x)
    B, S, _ = x.shape
    T = B * S
    x2d = x.reshape(T, D_MODEL)
    r3 = lambda t: t.reshape(B, S, -1)

    def cached(t, rows):
        g, w, c = t.shape
        t = t.reshape(g, w // HEAD_DIM, HEAD_DIM, c // rows, rows)
        return jnp.transpose(t, (0, 3, 4, 1, 2)).reshape(B, rows, w // HEAD_DIM, HEAD_DIM)

    if cache is None:
        tm = min(S, PROJ_ROWS)
        la = min(LEFT_CHUNKS_A * CHUNK, S)
        assert S % tm == 0 and la <= tm
        proj = _proj_ab(x2d, g_pre[None, :], w_in.astype(BF16), tm, S // tm, la)
        qa, ka, va, sga, qb, kb, vb, sgb = proj[:8]
        oa = _attn_a(r3(qa), r3(ka), r3(va), r3(sga), _bias_table(table, A_WIN), window=False)
        ob = _attn_b(r3(qb), r3(kb), r3(vb), r3(sgb))
    else:
        ca_k, ca_v, cb_k, cb_v = cache
        la = ca_k.shape[1]
        assert la == LEFT_CHUNKS_A * CHUNK and S <= CHUNK and cb_k.shape[1] % SB_BLK == 0
        keys_minor = lambda c: jnp.transpose(c, (0, 2, 3, 1)).reshape(B, c.shape[2] * HEAD_DIM, c.shape[1])
        tm = T
        proj = _proj_ab(x2d, g_pre[None, :], w_in.astype(BF16), tm, 1, tm)
        qa, ka, va, sga, qb, kb, vb, sgb = proj[:8]

        def window(c, new):
            parts = [keys_minor(c).astype(BF16), jnp.swapaxes(r3(new), 1, 2)]
            return jnp.pad(jnp.concatenate(parts, axis=2), ((0, 0), (0, 0), (0, A_WIN - la - S)))

        oa = _attn_a(r3(qa), window(ca_k, ka), window(ca_v, va), r3(sga), _bias_table(table, la + S), window=True)
        ob = _attn_b_cached(r3(qb), r3(kb), r3(vb), keys_minor(cb_k), keys_minor(cb_v), r3(sgb))
    rows = (cached(proj[8], min(la, S)), cached(proj[9], min(la, S)), cached(proj[10], S), cached(proj[11], S))
    merge = ([oa.reshape(T, W_A), ob.reshape(T, W_B)], w_out.astype(BF16), g_post[None, :], x2d)
    return (merge, B, S, tm), rows


def _c_layer(x, cache, past, g_pre, w_in, sinks, w_out, g_post):
    if isinstance(x, tuple):
        merge, B, S, _ = x
        x2d = None
    else:
        merge, (B, S, _) = None, x.shape
        x2d = x.reshape(B * S, D_MODEL)
    T = B * S
    g0 = W_C + 2 * KVW_C
    w_perm = jnp.concatenate([_c_head_order(w_in[:, :W_C], 1), w_in[:, W_C:g0], _c_head_order(w_in[:, g0:], 1)],
                             axis=1)
    r3 = lambda t: t.reshape(B, S, -1)
    if cache is None:
        tm = min(S, PROJ_ROWS)
        lc = min(WINDOW_C, S)
        assert S % tm == 0 and lc <= tm
        cos, sin = _rope_tables(S, 0, S)
        proj = _proj_c(x2d, g_pre[None, :], w_perm.astype(BF16), cos, sin, tm, S // tm, lc, merge)
        x2d, (q, k, v, sg, kf, vf) = (proj[0], proj[1:]) if merge is not None else (x2d, proj)
        o = _attn_c(r3(q), r3(k), r3(v), r3(sg), sinks)
    else:
        tm = T
        cos, sin = _rope_tables(T, past, S)
        proj = _proj_c(x2d, g_pre[None, :], w_perm.astype(BF16), cos, sin, tm, 1, tm, merge)
        x2d, (q, k, v, sg, kf, vf) = (proj[0], proj[1:]) if merge is not None else (x2d, proj)
        cc_k, cc_v = cache
        lc = cc_k.shape[1]
        assert lc == WINDOW_C and S <= CHUNK

        def window(c, new):
            old = jnp.transpose(c, (0, 2, 3, 1)).reshape(B, KVW_C, lc).astype(BF16)
            parts = [old, jnp.swapaxes(r3(new), 1, 2)]
            return jnp.pad(jnp.concatenate(parts, axis=2), ((0, 0), (0, 0), (0, C_WIN - lc - S)))

        o = _attn_c(r3(q), window(cc_k, k), window(cc_v, v), r3(sg), sinks, n_valid=lc + S)
    rows = (_heads(kf, B, KV_C), _heads(vf, B, KV_C))
    merge = ([o.reshape(T, W_C)], _c_head_order(w_out, 0).astype(BF16), g_post[None, :], x2d)
    return (merge, B, S, tm), rows


def kernel(x_prompt, x_sample, cache_a_k, cache_a_v, cache_b_k, cache_b_v, cache_c_k, cache_c_v,
           ab_norm_pre, ab_w_in, ab_w_out, ab_norm_post, a_rel_bias,
           c_norm_pre, c_w_in, c_sinks, c_w_out, c_norm_post):
    past = cache_b_k.shape[2]
    depth = ab_w_in.shape[0] + c_w_in.shape[0]
    yp, ys = x_prompt, x_sample
    ab_p, ab_s, c_p, c_s = [], [], [], []
    for layer in range(depth):
        i = layer // 2
        if layer % 2 == 0:
            w = (ab_norm_pre[i], ab_w_in[i], ab_w_out[i], ab_norm_post[i], a_rel_bias[i])
            yp, rows = _ab_layer(yp, None, *w)
            ab_p.append(rows)
            ys, rows = _ab_layer(ys, (cache_a_k[i], cache_a_v[i], cache_b_k[i], cache_b_v[i]), *w)
            ab_s.append(rows)
        else:
            w = (c_norm_pre[i], c_w_in[i], c_sinks[i], c_w_out[i], c_norm_post[i])
            yp, rows = _c_layer(yp, None, past, *w)
            c_p.append(rows)
            ys, rows = _c_layer(ys, (cache_c_k[i], cache_c_v[i]), past, *w)
            c_s.append(rows)
    st = lambda rows, j: jnp.stack([r[j] for r in rows])
    return (_flush(yp), _flush(ys),
            st(ab_p, 0), st(ab_p, 1), st(ab_p, 2), st(ab_p, 3), st(c_p, 0), st(c_p, 1),
            st(ab_s, 0), st(ab_s, 1), st(ab_s, 2), st(ab_s, 3), st(c_s, 0), st(c_s, 1))
```
